```python
import jax, jax.numpy as jnp
from jax import lax
import numpy as np

D_MODEL = 1024
BATCH = 4
SEQ = 4096
DEPTH = 4
DEC_BATCH = 128
DEC_SEQ = 4
PAST_LEN = 8192
PAGE_SIZE = 128

D_RG = 512
N_RG_HEADS = 8
RG_HEAD_DIM = D_RG // N_RG_HEADS
CONV_W = 4
LRU_C = 8.0
N_HEADS = 4
QK_NOPE = 128
QK_ROPE = 64
V_DIM = 128
D_ATT = N_HEADS * V_DIM
Q_RANK = 256
KV_RANK = 256
ROPE_THETA = 10000.0
ATT_SCALE = (QK_NOPE + QK_ROPE) ** -0.5
Q_BLOCK = 128
D_MIX = D_RG + D_ATT
D_IN = 2 * D_RG + Q_RANK + KV_RANK + QK_ROPE + D_ATT
SPLITS = [D_RG, 2 * D_RG, 2 * D_RG + Q_RANK, 2 * D_RG + Q_RANK + KV_RANK,
          2 * D_RG + Q_RANK + KV_RANK + QK_ROPE]
EPS = 1e-6

kernel_name = "hymba_rglru_mla_decoder_step"


def rmsnorm(x, g):
    xf = x.astype(jnp.float32)
    y = xf * lax.rsqrt(jnp.mean(xf * xf, axis=-1, keepdims=True) + EPS)
    return (y * g.astype(jnp.float32)).astype(x.dtype)


def rope(x, pos):
    half = QK_ROPE // 2
    freqs = ROPE_THETA ** (-jnp.arange(half, dtype=jnp.float32) / half)
    ang = pos.astype(jnp.float32)[:, None] * freqs
    ang = ang.reshape((1, ang.shape[0]) + (1,) * (x.ndim - 3) + (half,))
    cos, sin = jnp.cos(ang), jnp.sin(ang)
    xf = x.astype(jnp.float32)
    x1, x2 = xf[..., :half], xf[..., half:]
    return jnp.concatenate([x1 * cos - x2 * sin, x2 * cos + x1 * sin], axis=-1).astype(x.dtype)


def causal_conv(x_r, buf, w, b):
    T = x_r.shape[1]
    xp = jnp.concatenate([buf.astype(x_r.dtype), x_r], axis=1)
    y = b
    for k in range(CONV_W):
        y = y + xp[:, k:k + T] * w[k]
    return y, xp[:, -(CONV_W - 1):]


def rglru(x, h0, w_a, b_a, w_x, b_x, lam):
    B, T, _ = x.shape
    xh = x.reshape(B, T, N_RG_HEADS, RG_HEAD_DIM)
    r = jax.nn.sigmoid((jnp.einsum('bthi,hij->bthj', xh, w_a).reshape(B, T, D_RG) + b_a).astype(jnp.float32))
    i = jax.nn.sigmoid((jnp.einsum('bthi,hij->bthj', xh, w_x).reshape(B, T, D_RG) + b_x).astype(jnp.float32))
    log_a = -LRU_C * r * jax.nn.softplus(-lam.astype(jnp.float32))
    a = jnp.exp(log_a)
    inp = jnp.sqrt(-jnp.expm1(2.0 * log_a)) * i * x.astype(jnp.float32)

    def step(h, ab):
        a_t, b_t = ab
        h = a_t * h + b_t
        return h, h

    h_last, hs = lax.scan(step, h0.astype(jnp.float32), (jnp.swapaxes(a, 0, 1), jnp.swapaxes(inp, 0, 1)))
    return jnp.swapaxes(hs, 0, 1).astype(x.dtype), h_last.astype(x.dtype)


def latent_attention(q_lat, q_pe, q_pos, c_all, kpe_all, k_pos):
    s = (jnp.einsum('bthr,bsr->bhts', q_lat, c_all) + jnp.einsum('bthd,bsd->bhts', q_pe, kpe_all))
    s = s.astype(jnp.float32) * ATT_SCALE
    mask = k_pos[None, :] <= q_pos[:, None]
    s = jnp.where(mask[None, None], s, -1e30)
    p = jax.nn.softmax(s, axis=-1).astype(c_all.dtype)
    return jnp.einsum('bhts,bsr->bthr', p, c_all)


def mixer_layer(x, pos, conv_buf, h0, attend, norm_g, w_in, conv_w, conv_b, w_a, b_a, w_x, b_x,
                lam, q_norm_g, w_uq, kv_norm_g, w_uk, w_uv, w_out):
    B, T, _ = x.shape
    u = rmsnorm(x, norm_g)
    z = u @ w_in
    x_r, g_r, c_q, c_kv, k_pe_raw, g_a = jnp.split(z, SPLITS, axis=-1)
    xc, new_buf = causal_conv(x_r, conv_buf, conv_w, conv_b)
    y_r, h_last = rglru(xc, h0, w_a, b_a, w_x, b_x, lam)
    y_r = y_r * jax.nn.silu(g_r)
    q = (rmsnorm(c_q, q_norm_g) @ w_uq).reshape(B, T, N_HEADS, QK_NOPE + QK_ROPE)
    q_nope, q_pe = q[..., :QK_NOPE], rope(q[..., QK_NOPE:], pos)
    q_lat = jnp.einsum('bthd,rhd->bthr', q_nope, w_uk)
    c_kv_n = rmsnorm(c_kv, kv_norm_g)
    k_pe = rope(k_pe_raw, pos)
    o_lat = attend(q_lat, q_pe, c_kv_n, k_pe)
    y_a = jnp.einsum('bthr,rhv->bthv', o_lat, w_uv).reshape(B, T, D_ATT) * jax.nn.silu(g_a)
    out = jnp.concatenate([y_r, y_a], axis=-1) @ w_out
    return x + out, c_kv_n, k_pe, h_last, new_buf


def setup_inputs(seed: int = 0) -> dict:
    key = jax.random.key(seed)
    ks = jax.random.split(key, 24)
    f32 = jnp.float32
    n_pages = PAST_LEN // PAGE_SIZE
    n_used = DEC_BATCH * n_pages
    n_pool = n_used + n_used // 4
    nrm = lambda k, shp, s: jax.random.normal(k, shp, f32) * s
    u = jax.random.uniform(ks[0], (DEPTH, D_RG), f32, minval=0.9, maxval=0.999)
    a = u ** (1.0 / LRU_C)
    lam = jnp.log(a) - jnp.log1p(-a)
    page_table = jax.random.permutation(ks[1], n_pool)[:n_used].reshape(DEC_BATCH, n_pages).astype(jnp.int32)
    return {
        "x_prompt": nrm(ks[2], (BATCH, SEQ, D_MODEL), 1.0),
        "x_sample": nrm(ks[3], (DEC_BATCH, DEC_SEQ, D_MODEL), 1.0),
        "cache_kv_latent": nrm(ks[4], (DEPTH, n_pool, PAGE_SIZE, KV_RANK), 1.0),
        "cache_k_rope": nrm(ks[5], (DEPTH, n_pool, PAGE_SIZE, QK_ROPE), 1.0),
        "page_table": page_table,
        "state_rglru": nrm(ks[6], (DEPTH, DEC_BATCH, D_RG), 0.5),
        "state_conv": nrm(ks[7], (DEPTH, DEC_BATCH, CONV_W - 1, D_RG), 1.0),
        "norm_g": 1.0 + nrm(ks[8], (DEPTH, D_MODEL), 0.01),
        "w_in": nrm(ks[9], (DEPTH, D_MODEL, D_IN), D_MODEL ** -0.5),
        "conv_w": nrm(ks[10], (DEPTH, CONV_W, D_RG), CONV_W ** -0.5),
        "conv_b": nrm(ks[11], (DEPTH, D_RG), 0.01),
        "w_a": nrm(ks[12], (DEPTH, N_RG_HEADS, RG_HEAD_DIM, RG_HEAD_DIM), RG_HEAD_DIM ** -0.5),
        "b_a": nrm(ks[13], (DEPTH, D_RG), 0.01),
        "w_x": nrm(ks[14], (DEPTH, N_RG_HEADS, RG_HEAD_DIM, RG_HEAD_DIM), RG_HEAD_DIM ** -0.5),
        "b_x": nrm(ks[15], (DEPTH, D_RG), 0.01),
        "lru_lambda": lam,
        "q_norm_g": 1.0 + nrm(ks[16], (DEPTH, Q_RANK), 0.01),
        "w_uq": nrm(ks[17], (DEPTH, Q_RANK, N_HEADS * (QK_NOPE + QK_ROPE)), Q_RANK ** -0.5),
        "kv_norm_g": 1.0 + nrm(ks[18], (DEPTH, KV_RANK), 0.01),
        "w_uk": nrm(ks[19], (DEPTH, KV_RANK, N_HEADS, QK_NOPE), KV_RANK ** -0.5),
        "w_uv": nrm(ks[20], (DEPTH, KV_RANK, N_HEADS, V_DIM), KV_RANK ** -0.5),
        "w_out": nrm(ks[21], (DEPTH, D_MIX, D_MODEL), D_MIX ** -0.5),
        "final_norm_g": 1.0 + nrm(ks[22], (D_MODEL,), 0.01),
    }


def reference(x_prompt, x_sample, cache_kv_latent, cache_k_rope, page_table, state_rglru, state_conv,
              norm_g, w_in, conv_w, conv_b, w_a, b_a, w_x, b_x, lru_lambda, q_norm_g, w_uq,
              kv_norm_g, w_uk, w_uv, w_out, final_norm_g):
    Bp, T = x_prompt.shape[:2]
    Bs, Ts = x_sample.shape[:2]
    n_pages = page_table.shape[1]
    past_len = n_pages * cache_kv_latent.shape[2]
    pos_p = jnp.arange(T, dtype=jnp.int32)
    pos_s = past_len + jnp.arange(Ts, dtype=jnp.int32)
    k_pos_s = jnp.arange(past_len + Ts, dtype=jnp.int32)
    n_blk = T // Q_BLOCK

    def prompt_attend(q_lat, q_pe, c_kv, k_pe):
        qb = jnp.swapaxes(q_lat.reshape(Bp, n_blk, Q_BLOCK, N_HEADS, KV_RANK), 0, 1)
        pb = jnp.swapaxes(q_pe.reshape(Bp, n_blk, Q_BLOCK, N_HEADS, QK_ROPE), 0, 1)
        posb = pos_p.reshape(n_blk, Q_BLOCK)
        o = lax.map(lambda a: latent_attention(a[0], a[1], a[2], c_kv, k_pe, pos_p), (qb, pb, posb))
        return jnp.swapaxes(o, 0, 1).reshape(Bp, T, N_HEADS, KV_RANK)

    x_p, x_s = x_prompt, x_sample
    kv_p, kr_p, h_p, cv_p = [], [], [], []
    kv_s, kr_s, h_s, cv_s = [], [], [], []
    for l in range(DEPTH):
        lw = (norm_g[l], w_in[l], conv_w[l], conv_b[l], w_a[l], b_a[l], w_x[l], b_x[l], lru_lambda[l],
              q_norm_g[l], w_uq[l], kv_norm_g[l], w_uk[l], w_uv[l], w_out[l])
        zero_buf = jnp.zeros((Bp, CONV_W - 1, D_RG), x_p.dtype)
        zero_h = jnp.zeros((Bp, D_RG), x_p.dtype)
        x_p, c1, k1, hl1, b1 = mixer_layer(x_p, pos_p, zero_buf, zero_h, prompt_attend, *lw)
        kv_p.append(c1); kr_p.append(k1); h_p.append(hl1); cv_p.append(b1)
        past_c = cache_kv_latent[l][page_table].reshape(Bs, past_len, KV_RANK)
        past_k = cache_k_rope[l][page_table].reshape(Bs, past_len, QK_ROPE)

        def sample_attend(q_lat, q_pe, c_kv, k_pe, past_c=past_c, past_k=past_k):
            c_all = jnp.concatenate([past_c.astype(c_kv.dtype), c_kv], axis=1)
            k_all = jnp.concatenate([past_k.astype(k_pe.dtype), k_pe], axis=1)
            return latent_attention(q_lat, q_pe, pos_s, c_all, k_all, k_pos_s)

        x_s, c2, k2, hl2, b2 = mixer_layer(x_s, pos_s, state_conv[l], state_rglru[l], sample_attend, *lw)
        kv_s.append(c2); kr_s.append(k2); h_s.append(hl2); cv_s.append(b2)

    y_prompt = rmsnorm(x_p, final_norm_g)
    y_sample = rmsnorm(x_s, final_norm_g)
    return (y_prompt, y_sample,
            jnp.stack(kv_p), jnp.stack(kr_p), jnp.stack(h_p), jnp.stack(cv_p),
            jnp.stack(kv_s), jnp.stack(kr_s), jnp.stack(h_s), jnp.stack(cv_s))
```

```python
import functools

import jax
import jax.numpy as jnp
from jax import lax
from jax.experimental import pallas as pl
from jax.experimental.pallas import tpu as pltpu

F32 = jnp.float32
BF16 = jnp.bfloat16

N_RG_HEADS = 8
CONV_W = 4
LRU_C = 8.0
N_HEADS = 4
QK_NOPE = 128
QK_ROPE = 64
V_DIM = 128
ROPE_THETA = 10000.0
ATT_SCALE = (QK_NOPE + QK_ROPE) ** -0.5
EPS = 1e-6
MASK_VALUE = -1e30

VMEM_LIMIT_BYTES = 48 * 1024 * 1024

ROW_TILE = 512
LRU_CHUNK = 512
ATT_TQ = 512
ATT_TK = 512
PAGES_PER_STEP = 16


def _params(*sem):
    return pltpu.CompilerParams(dimension_semantics=sem, vmem_limit_bytes=VMEM_LIMIT_BYTES)


def _rms(v, g):
    return v * lax.rsqrt(jnp.mean(v * v, axis=-1, keepdims=True) + EPS) * g


def _silu(v):
    return v * jax.nn.sigmoid(v)


def _dot(a, b):
    return jnp.dot(a, b, preferred_element_type=F32)


def _dot_nt(a, b):
    return lax.dot_general(a, b, (((1,), (1,)), ((), ())), preferred_element_type=F32)


def _proj_kernel(x_ref, ng_ref, win_ref, qng_ref, wuq_ref, kvng_ref, wukt_ref, cos_ref, sin_ref,
                 xr_ref, sgr_ref, sga_ref, q_ref, k_ref, ckv_ref, kpe_ref, *, d_rg, q_rank, kv_rank):
    u = _rms(x_ref[...], ng_ref[...])
    z = _dot(u.astype(BF16), win_ref[...])
    o = 0
    xr_ref[...] = z[:, o:o + d_rg]; o += d_rg
    sgr_ref[...] = _silu(z[:, o:o + d_rg]); o += d_rg
    c_q = z[:, o:o + q_rank]; o += q_rank
    c_kv = z[:, o:o + kv_rank]; o += kv_rank
    d_att = N_HEADS * V_DIM
    sga_ref[...] = _silu(z[:, o:o + d_att]); o += d_att
    kpe_raw = z[:, o:o + QK_ROPE]; o += QK_ROPE
    kpe_swp = z[:, o:o + QK_ROPE]

    cos = cos_ref[...]
    sin = sin_ref[...]

    ckv_n = _rms(c_kv, kvng_ref[...])
    kpe = kpe_raw * cos[:, :QK_ROPE] + kpe_swp * sin[:, :QK_ROPE]
    ckv_ref[...] = ckv_n
    kpe_ref[...] = kpe
    k_ref[:, 0:kv_rank] = ckv_n.astype(BF16)
    k_ref[:, kv_rank:kv_rank + QK_ROPE] = kpe.astype(BF16)

    q = _dot(_rms(c_q, qng_ref[...]).astype(BF16), wuq_ref[...])
    pe0 = N_HEADS * QK_NOPE
    sw0 = pe0 + N_HEADS * QK_ROPE
    for pair in range(N_HEADS // 2):
        pe = q[:, pe0 + 128 * pair:pe0 + 128 * (pair + 1)]
        sw = q[:, sw0 + 128 * pair:sw0 + 128 * (pair + 1)]
        pe_r = (pe * cos + sw * sin) * ATT_SCALE
        for sub in range(2):
            h = 2 * pair + sub
            q_ref[h, :, kv_rank:kv_rank + QK_ROPE] = pe_r[:, QK_ROPE * sub:QK_ROPE * (sub + 1)].astype(BF16)
    for h in range(N_HEADS):
        q_lat = _dot(q[:, QK_NOPE * h:QK_NOPE * (h + 1)].astype(BF16), wukt_ref[h])
        q_ref[h, :, 0:kv_rank] = (q_lat * ATT_SCALE).astype(BF16)


def _proj(x, lw, cos_tab, sin_tab, tab_blocks):
    n, d_model = x.shape
    tm = min(ROW_TILE, n)
    d_in = lw["w_in"].shape[1]
    d_rg = lw["conv_b"].shape[-1]
    q_rank = lw["q_norm_g"].shape[-1]
    kv_rank = lw["kv_norm_g"].shape[-1]
    d_att = N_HEADS * V_DIM
    d_qk = kv_rank + QK_ROPE
    const = lambda *shape: pl.BlockSpec(shape, lambda i: (0,) * len(shape))
    rows = lambda w: pl.BlockSpec((tm, w), lambda i: (i, 0))
    tab = pl.BlockSpec((tm, 128), lambda i: (i % tab_blocks, 0))
    kern = functools.partial(_proj_kernel, d_rg=d_rg, q_rank=q_rank, kv_rank=kv_rank)
    return pl.pallas_call(
        kern,
        grid=(n // tm,),
        in_specs=[rows(d_model), const(1, d_model), const(d_model, d_in), const(1, q_rank),
                  const(q_rank, lw["w_uq"].shape[1]), const(1, kv_rank),
                  const(N_HEADS, QK_NOPE, kv_rank), tab, tab],
        out_specs=[rows(d_rg), rows(d_rg), rows(d_att),
                   pl.BlockSpec((N_HEADS, tm, d_qk), lambda i: (0, i, 0)),
                   rows(d_qk), rows(kv_rank), rows(QK_ROPE)],
        out_shape=[jax.ShapeDtypeStruct((n, d_rg), F32), jax.ShapeDtypeStruct((n, d_rg), F32),
                   jax.ShapeDtypeStruct((n, d_att), F32),
                   jax.ShapeDtypeStruct((N_HEADS, n, d_qk), BF16),
                   jax.ShapeDtypeStruct((n, d_qk), BF16),
                   jax.ShapeDtypeStruct((n, kv_rank), F32), jax.ShapeDtypeStruct((n, QK_ROPE), F32)],
        compiler_params=_params("arbitrary"),
        name="proj",
    )(x, lw["norm_g"], lw["w_in"], lw["q_norm_g"], lw["w_uq"], lw["kv_norm_g"], lw["w_ukt"],
      cos_tab, sin_tab)


def _lru_gates(xc, wg_ref, ba_ref, bx_ref, lam_ref):
    d_rg = xc.shape[-1]
    g = _dot(xc.astype(BF16), wg_ref[...])
    r = jax.nn.sigmoid(g[:, :d_rg] + ba_ref[...])
    i = jax.nn.sigmoid(g[:, d_rg:] + bx_ref[...])
    neg_lam = -lam_ref[...]
    softplus = jnp.maximum(neg_lam, 0.0) + jnp.log1p(jnp.exp(-jnp.abs(neg_lam)))
    log_a = -LRU_C * r * softplus
    a = jnp.exp(log_a)
    return a, jnp.sqrt(-jnp.tanh(log_a) * (a * a + 1.0)) * i * xc


def _rglru_prompt_kernel(xr_ref, sgr_ref, cw_ref, cb_ref, wg_ref, ba_ref, bx_ref, lam_ref,
                         y_ref, hlast_ref, conv_ref, xpad_scr, a_scr, b_scr, h_scr, *, tc):
    ci = pl.program_id(1)
    d_rg = xr_ref.shape[-1]

    @pl.when(ci == 0)
    def _():
        xpad_scr[0:8, :] = jnp.zeros((8, d_rg), F32)
        h_scr[...] = jnp.zeros((8, d_rg), F32)

    xpad_scr[8:8 + tc, :] = xr_ref[...]
    xc = cb_ref[...] + cw_ref[0:1, :] * xpad_scr[5:5 + tc, :]
    for k in range(1, CONV_W):
        xc = xc + cw_ref[k:k + 1, :] * xpad_scr[5 + k:5 + k + tc, :]
    a, b = _lru_gates(xc, wg_ref, ba_ref, bx_ref, lam_ref)

    a = a.reshape(tc // 8, 8, d_rg)
    b = b.reshape(tc // 8, 8, d_rg)
    step = lax.broadcasted_iota(jnp.int32, a.shape, 1)
    for d in (1, 2, 4):
        keep = step >= d
        b = jnp.where(keep, a * pltpu.roll(b, d, 1) + b, b)
        a = jnp.where(keep, a * pltpu.roll(a, d, 1), a)
    a_scr[...] = a.reshape(tc, d_rg)
    b_scr[...] = b.reshape(tc, d_rg)

    def group(g, h_in):
        r0 = pl.multiple_of(g * 8, 8)
        hs = a_scr[pl.ds(r0, 8), :] * h_in + b_scr[pl.ds(r0, 8), :]
        b_scr[pl.ds(r0, 8), :] = hs
        return jnp.broadcast_to(hs[7:8, :], hs.shape)

    h_in = lax.fori_loop(0, tc // 8, group, h_scr[...], unroll=8)
    h_scr[...] = h_in
    y_ref[...] = (b_scr[...] * sgr_ref[...]).astype(y_ref.dtype)
    xpad_scr[0:8, :] = xpad_scr[tc:tc + 8, :]

    @pl.when(ci == pl.num_programs(1) - 1)
    def _():
        hlast_ref[...] = h_in[0:1, :]
        conv_ref[...] = xpad_scr[tc + 8 - (CONV_W - 1):tc + 8, :]


def _rglru_prompt(xr, sgr, lw, bsz, t):
    d_rg = xr.shape[-1]
    tc = min(LRU_CHUNK, t)
    nc = t // tc
    const = lambda *shape: pl.BlockSpec(shape, lambda b, c: (0,) * len(shape))
    rows = pl.BlockSpec((tc, d_rg), lambda b, c: (b * nc + c, 0))
    y, hlast, conv = pl.pallas_call(
        functools.partial(_rglru_prompt_kernel, tc=tc),
        grid=(bsz, nc),
        in_specs=[rows, rows, const(CONV_W, d_rg), const(1, d_rg), const(d_rg, 2 * d_rg),
                  const(1, d_rg), const(1, d_rg), const(1, d_rg)],
        out_specs=[rows, pl.BlockSpec((None, 1, d_rg), lambda b, c: (b, 0, 0)),
                   pl.BlockSpec((None, CONV_W - 1, d_rg), lambda b, c: (b, 0, 0))],
        out_shape=[jax.ShapeDtypeStruct((bsz * t, d_rg), BF16),
                   jax.ShapeDtypeStruct((bsz, 1, d_rg), F32),
                   jax.ShapeDtypeStruct((bsz, CONV_W - 1, d_rg), F32)],
        scratch_shapes=[pltpu.VMEM((tc + 8, d_rg), F32), pltpu.VMEM((tc, d_rg), F32),
                        pltpu.VMEM((tc, d_rg), F32), pltpu.VMEM((8, d_rg), F32)],
        compiler_params=_params("arbitrary", "arbitrary"),
        name="rglru_prompt",
    )(xr, sgr, lw["conv_w"], lw["conv_b"], lw["w_gate"], lw["b_a"], lw["b_x"], lw["lru_lambda"])
    return y, hlast.reshape(bsz, d_rg), conv


def _rglru_sample_kernel(xr_ref, sgr_ref, buf_ref, h0_ref, cw_ref, cb_ref, wg_ref, ba_ref, bx_ref,
                         lam_ref, y_ref, hlast_ref, conv_ref):
    t_len = xr_ref.shape[0]
    xp = [buf_ref[k] for k in range(CONV_W - 1)] + [xr_ref[t] for t in range(t_len)]
    h = h0_ref[...]
    for t in range(t_len):
        xc = cb_ref[...] + cw_ref[0:1, :] * xp[t]
        for k in range(1, CONV_W):
            xc = xc + cw_ref[k:k + 1, :] * xp[t + k]
        a, b = _lru_gates(xc, wg_ref, ba_ref, bx_ref, lam_ref)
        h = a * h + b
        y_ref[t] = (h * sgr_ref[t]).astype(y_ref.dtype)
    hlast_ref[...] = h
    for k in range(CONV_W - 1):
        conv_ref[k] = xp[t_len + k]


def _rglru_sample(xr_t, sgr_t, buf_t, h0, lw):
    t_len, bsz, d_rg = xr_t.shape
    return pl.pallas_call(
        _rglru_sample_kernel,
        out_shape=[jax.ShapeDtypeStruct((t_len, bsz, d_rg), BF16),
                   jax.ShapeDtypeStruct((bsz, d_rg), F32),
                   jax.ShapeDtypeStruct((CONV_W - 1, bsz, d_rg), F32)],
        compiler_params=pltpu.CompilerParams(vmem_limit_bytes=VMEM_LIMIT_BYTES),
        name="rglru_sample",
    )(xr_t, sgr_t, buf_t, h0, lw["conv_w"], lw["conv_b"], lw["w_gate"], lw["b_a"], lw["b_x"],
      lw["lru_lambda"])


def _softmax_step(s, v, m_scr, l_scr, acc_scr):
    m_prev = m_scr[...]
    m_new = jnp.maximum(m_prev, jnp.max(s, axis=-1, keepdims=True))
    alpha = jnp.exp(m_prev - m_new)
    p = jnp.exp(s - m_new)
    l_scr[...] = alpha * l_scr[...] + jnp.sum(p, axis=-1, keepdims=True)
    acc_scr[...] = alpha * acc_scr[...] + _dot(p.astype(BF16), v)
    m_scr[...] = m_new


def _softmax_init(m_scr, l_scr, acc_scr):
    m_scr[...] = jnp.full(m_scr.shape, MASK_VALUE, F32)
    l_scr[...] = jnp.zeros(l_scr.shape, F32)
    acc_scr[...] = jnp.zeros(acc_scr.shape, F32)


def _attn_prompt_kernel(q_ref, k_ref, o_ref, m_scr, l_scr, acc_scr, *, tq, tk, kv_rank):
    qi = pl.program_id(1)
    rows = N_HEADS * tq
    q = q_ref[...].reshape(rows, q_ref.shape[-1])
    _softmax_init(m_scr, l_scr, acc_scr)

    def block(ki, masked):
        k = k_ref[pl.ds(pl.multiple_of(ki * tk, tk), tk), :]
        s = _dot_nt(q, k)
        if masked:
            q_pos = qi * tq + (lax.broadcasted_iota(jnp.int32, s.shape, 0) & (tq - 1))
            k_pos = ki * tk + lax.broadcasted_iota(jnp.int32, s.shape, 1)
            s = jnp.where(k_pos <= q_pos, s, MASK_VALUE)
        _softmax_step(s, k[:, :kv_rank], m_scr, l_scr, acc_scr)

    n_full = (qi * tq) // tk
    n_all = ((qi + 1) * tq + tk - 1) // tk

    def full_body(ki, c):
        block(ki, False)
        return c

    def diag_body(ki, c):
        block(ki, True)
        return c

    lax.fori_loop(0, n_full, full_body, 0)
    lax.fori_loop(n_full, n_all, diag_body, 0)
    o = acc_scr[...] / l_scr[...]
    o_ref[...] = o.reshape(N_HEADS, tq, kv_rank).astype(o_ref.dtype)


def _attn_prompt(q, k, bsz, t, kv_rank):
    d_qk = q.shape[-1]
    tq = min(ATT_TQ, t)
    tk = min(ATT_TK, t)
    nq = t // tq
    rows = N_HEADS * tq
    return pl.pallas_call(
        functools.partial(_attn_prompt_kernel, tq=tq, tk=tk, kv_rank=kv_rank),
        grid=(bsz, nq),
        in_specs=[pl.BlockSpec((N_HEADS, tq, d_qk), lambda b, i: (0, b * nq + i, 0)),
                  pl.BlockSpec((t, d_qk), lambda b, i: (b, 0))],
        out_specs=pl.BlockSpec((N_HEADS, tq, kv_rank), lambda b, i: (0, b * nq + i, 0)),
        out_shape=jax.ShapeDtypeStruct((N_HEADS, bsz * t, kv_rank), BF16),
        scratch_shapes=[pltpu.VMEM((rows, 1), F32), pltpu.VMEM((rows, 1), F32),
                        pltpu.VMEM((rows, kv_rank), F32)],
        compiler_params=_params("arbitrary", "arbitrary"),
        name="attn_prompt",
    )(q, k)


def _attn_sample_kernel(pt_ref, q_ref, ccur_ref, pcur_ref, *rest, n_pg, n_chunks, t_len, page, kv_rank):
    kv_refs = rest[:n_pg]
    kr_refs = rest[n_pg:2 * n_pg]
    o_ref = rest[2 * n_pg]
    kc_scr, kp_scr, cc_scr, pc_scr, m_scr, l_scr, acc_scr = rest[2 * n_pg + 1:]
    c = pl.program_id(1)

    @pl.when(c == 0)
    def _():
        _softmax_init(m_scr, l_scr, acc_scr)

    for j in range(n_pg):
        kc_scr[j * page:(j + 1) * page, :] = kv_refs[j][...].astype(BF16)
        kp_scr[j * page:(j + 1) * page, :] = kr_refs[j][...].astype(BF16)
    q = q_ref[...]
    q_lat = q[:, :kv_rank]
    q_pe = q[:, kv_rank:]
    kc = kc_scr[...]
    s = _dot_nt(q_lat, kc) + _dot_nt(q_pe, kp_scr[...])
    _softmax_step(s, kc, m_scr, l_scr, acc_scr)

    @pl.when(c == n_chunks - 1)
    def _():
        cc_scr[...] = jnp.zeros(cc_scr.shape, F32)
        pc_scr[...] = jnp.zeros(pc_scr.shape, F32)
        cc_scr[0:t_len, :] = ccur_ref[...]
        pc_scr[0:t_len, :] = pcur_ref[...]
        vc = cc_scr[...].astype(BF16)
        s_cur = _dot_nt(q_lat, vc) + _dot_nt(q_pe, pc_scr[...].astype(BF16))
        q_step = lax.broadcasted_iota(jnp.int32, s_cur.shape, 0) % t_len
        k_step = lax.broadcasted_iota(jnp.int32, s_cur.shape, 1)
        s_cur = jnp.where(k_step <= q_step, s_cur, MASK_VALUE)
        _softmax_step(s_cur, vc, m_scr, l_scr, acc_scr)
        o_ref[...] = (acc_scr[...] / l_scr[...]).astype(o_ref.dtype)


def _attn_sample(q, ckv, kpe, cache_kv, cache_kr, page_table, layer):
    bsz, rows, d_qk = q.shape
    t_len = ckv.shape[1]
    kv_rank = ckv.shape[-1]
    page = cache_kv.shape[2]
    n_pages = page_table.shape[1]
    n_pg = min(PAGES_PER_STEP, n_pages)
    n_chunks = n_pages // n_pg

    def page_spec(width, j):
        return pl.BlockSpec((None, None, page, width),
                            lambda b, c, pt: (layer, pt[b, c * n_pg + j], 0, 0))

    in_specs = [pl.BlockSpec((None, rows, d_qk), lambda b, c, pt: (b, 0, 0)),
                pl.BlockSpec((None, t_len, kv_rank), lambda b, c, pt: (b, 0, 0)),
                pl.BlockSpec((None, t_len, QK_ROPE), lambda b, c, pt: (b, 0, 0))]
    in_specs += [page_spec(kv_rank, j) for j in range(n_pg)]
    in_specs += [page_spec(QK_ROPE, j) for j in range(n_pg)]
    kern = functools.partial(_attn_sample_kernel, n_pg=n_pg, n_chunks=n_chunks, t_len=t_len,
                             page=page, kv_rank=kv_rank)
    return pl.pallas_call(
        kern,
        grid_spec=pltpu.PrefetchScalarGridSpec(
            num_scalar_prefetch=1,
            grid=(bsz, n_chunks),
            in_specs=in_specs,
            out_specs=pl.BlockSpec((None, rows, kv_rank), lambda b, c, pt: (b, 0, 0)),
            scratch_shapes=[pltpu.VMEM((n_pg * page, kv_rank), BF16),
                            pltpu.VMEM((n_pg * page, QK_ROPE), BF16),
                            pltpu.VMEM((page, kv_rank), F32), pltpu.VMEM((page, QK_ROPE), F32),
                            pltpu.VMEM((rows, 1), F32), pltpu.VMEM((rows, 1), F32),
                            pltpu.VMEM((rows, kv_rank), F32)]),
        out_shape=jax.ShapeDtypeStruct((bsz, rows, kv_rank), BF16),
        compiler_params=_params("arbitrary", "arbitrary"),
        name="attn_sample",
    )(page_table, q, ckv, kpe, *([cache_kv] * n_pg), *([cache_kr] * n_pg))


def _out_kernel(x_ref, yr_ref, o_ref, sga_ref, wuv_ref, wout_ref, fg_ref, out_ref, *, final):
    d_rg = yr_ref.shape[-1]
    sga = sga_ref[...]
    acc = x_ref[...] + _dot(yr_ref[...], wout_ref[0:d_rg, :])
    y_a = [(_dot(o_ref[h], wuv_ref[h]) * sga[:, V_DIM * h:V_DIM * (h + 1)]).astype(BF16)
           for h in range(N_HEADS)]
    acc = acc + _dot(jnp.concatenate(y_a, axis=-1), wout_ref[d_rg:, :])
    out_ref[...] = _rms(acc, fg_ref[...]) if final else acc


def _out(x, yr, o_lat, sga, lw, final_g, final):
    n, d_model = x.shape
    tm = min(ROW_TILE, n)
    d_rg = yr.shape[-1]
    kv_rank = o_lat.shape[-1]
    d_att = sga.shape[-1]
    const = lambda *shape: pl.BlockSpec(shape, lambda i: (0,) * len(shape))
    rows = lambda w: pl.BlockSpec((tm, w), lambda i: (i, 0))
    return pl.pallas_call(
        functools.partial(_out_kernel, final=final),
        grid=(n // tm,),
        in_specs=[rows(d_model), rows(d_rg), pl.BlockSpec((N_HEADS, tm, kv_rank), lambda i: (0, i, 0)),
                  rows(d_att), const(N_HEADS, kv_rank, V_DIM), const(d_rg + d_att, d_model),
                  const(1, d_model)],
        out_specs=rows(d_model),
        out_shape=jax.ShapeDtypeStruct((n, d_model), F32),
        compiler_params=_params("arbitrary"),
        name="out",
    )(x, yr, o_lat, sga, lw["w_uv"], lw["w_out"], final_g)


def _rope_tables(pos):
    half = QK_ROPE // 2
    freqs = ROPE_THETA ** (-jnp.arange(half, dtype=F32) / half)
    ang = pos.astype(F32)[:, None] * freqs
    cos, sin = jnp.cos(ang), jnp.sin(ang)
    return jnp.concatenate([cos, cos, cos, cos], axis=-1), jnp.concatenate([-sin, sin, -sin, sin], axis=-1)


def _swap_halves(w):
    half = w.shape[-1] // 2
    return jnp.concatenate([w[..., half:], w[..., :half]], axis=-1)


def _prepare_weights(norm_g, w_in, conv_w, conv_b, w_a, b_a, w_x, b_x, lru_lambda, q_norm_g, w_uq,
                     kv_norm_g, w_uk, w_uv, w_out):
    depth, d_model, _ = w_in.shape
    d_rg = conv_b.shape[-1]
    q_rank = q_norm_g.shape[-1]
    kv_rank = kv_norm_g.shape[-1]
    s0, s1, s2, s3 = d_rg, 2 * d_rg, 2 * d_rg + q_rank, 2 * d_rg + q_rank + kv_rank
    s4 = s3 + QK_ROPE
    k_pe_cols = w_in[..., s3:s4]
    w_in2 = jnp.concatenate([w_in[..., :s3], w_in[..., s4:], k_pe_cols, _swap_halves(k_pe_cols)], axis=-1)
    wq = w_uq.reshape(depth, q_rank, N_HEADS, QK_NOPE + QK_ROPE)
    q_pe_cols = wq[..., QK_NOPE:]
    w_uq2 = jnp.concatenate([wq[..., :QK_NOPE].reshape(depth, q_rank, -1),
                             q_pe_cols.reshape(depth, q_rank, -1),
                             _swap_halves(q_pe_cols).reshape(depth, q_rank, -1)], axis=-1)
    eye = jnp.eye(N_RG_HEADS, dtype=w_a.dtype)
    block_diag = lambda w: jnp.einsum("lhij,hg->lhigj", w, eye).reshape(depth, d_rg, d_rg)
    w_gate = jnp.concatenate([block_diag(w_a), block_diag(w_x)], axis=-1)
    row = lambda v: v[:, None, :]
    return dict(
        norm_g=row(norm_g), w_in=w_in2.astype(BF16), conv_w=conv_w, conv_b=row(conv_b),
        w_gate=w_gate.astype(BF16), b_a=row(b_a), b_x=row(b_x), lru_lambda=row(lru_lambda),
        q_norm_g=row(q_norm_g), w_uq=w_uq2.astype(BF16), kv_norm_g=row(kv_norm_g),
        w_ukt=jnp.transpose(w_uk, (0, 2, 3, 1)).astype(BF16),
        w_uv=jnp.transpose(w_uv, (0, 2, 1, 3)).astype(BF16),
        w_out=w_out.astype(BF16))


def kernel(x_prompt, x_sample, cache_kv_latent, cache_k_rope, page_table, state_rglru, state_conv,
           norm_g, w_in, conv_w, conv_b, w_a, b_a, w_x, b_x, lru_lambda, q_norm_g, w_uq,
           kv_norm_g, w_uk, w_uv, w_out, final_norm_g):
    bp, t, d_model = x_prompt.shape
    bs, ts, _ = x_sample.shape
    depth = w_in.shape[0]
    kv_rank = kv_norm_g.shape[-1]
    d_rg = conv_b.shape[-1]
    past_len = page_table.shape[1] * cache_kv_latent.shape[2]

    weights = _prepare_weights(norm_g, w_in, conv_w, conv_b, w_a, b_a, w_x, b_x, lru_lambda,
                               q_norm_g, w_uq, kv_norm_g, w_uk, w_uv, w_out)
    final_g = final_norm_g[None, :]
    cos_p, sin_p = _rope_tables(jnp.arange(t, dtype=jnp.int32))
    pos_s = past_len + jnp.arange(ts, dtype=jnp.int32)
    cos_s, sin_s = _rope_tables(jnp.tile(pos_s, bs))
    tab_blocks_p = t // min(ROW_TILE, bp * t)
    tab_blocks_s = (bs * ts) // min(ROW_TILE, bs * ts)
    buf_t = jnp.transpose(state_conv, (0, 2, 1, 3))

    x_p = x_prompt.reshape(bp * t, d_model)
    x_s = x_sample.reshape(bs * ts, d_model)
    kv_p, kr_p, h_p, cv_p = [], [], [], []
    kv_s, kr_s, h_s, cv_s = [], [], [], []
    for l in range(depth):
        lw = {name: w[l] for name, w in weights.items()}
        final = l == depth - 1

        xr, sgr, sga, q, k, ckv, kpe = _proj(x_p, lw, cos_p, sin_p, tab_blocks_p)
        yr, hl, cv = _rglru_prompt(xr, sgr, lw, bp, t)
        o_lat = _attn_prompt(q, k, bp, t, kv_rank)
        x_p = _out(x_p, yr, o_lat, sga, lw, final_g, final)
        kv_p.append(ckv.reshape(bp, t, kv_rank)); kr_p.append(kpe.reshape(bp, t, QK_ROPE))
        h_p.append(hl); cv_p.append(cv)

        xr, sgr, sga, q, _, ckv, kpe = _proj(x_s, lw, cos_s, sin_s, tab_blocks_s)
        to_time_major = lambda v: jnp.transpose(v.reshape(bs, ts, -1), (1, 0, 2))
        yr_t, hl, cv_t = _rglru_sample(to_time_major(xr), to_time_major(sgr), buf_t[l],
                                       state_rglru[l], lw)
        yr = jnp.transpose(yr_t, (1, 0, 2)).reshape(bs * ts, d_rg)
        ckv = ckv.reshape(bs, ts, kv_rank)
        kpe = kpe.reshape(bs, ts, QK_ROPE)
        q_b = jnp.transpose(q.reshape(N_HEADS, bs, ts, -1), (1, 0, 2, 3)).reshape(bs, N_HEADS * ts, -1)
        o_b = _attn_sample(q_b, ckv, kpe, cache_kv_latent, cache_k_rope, page_table, l)
        o_lat = jnp.transpose(o_b.reshape(bs, N_HEADS, ts, kv_rank), (1, 0, 2, 3)).reshape(
            N_HEADS, bs * ts, kv_rank)
        x_s = _out(x_s, yr, o_lat, sga, lw, final_g, final)
        kv_s.append(ckv); kr_s.append(kpe); h_s.append(hl); cv_s.append(cv_t)

    y_prompt = x_p.reshape(bp, t, d_model)
    y_sample = x_s.reshape(bs, ts, d_model)
    conv_sample = jnp.transpose(jnp.stack(cv_s), (0, 2, 1, 3))
    return (y_prompt, y_sample,
            jnp.stack(kv_p), jnp.stack(kr_p), jnp.stack(h_p), jnp.stack(cv_p),
            jnp.stack(kv_s), jnp.stack(kr_s), jnp.stack(h_s), conv_sample)
```

```python
import functools

import jax
import jax.numpy as jnp
from jax import lax
from jax.experimental import pallas as pl
from jax.experimental.pallas import tpu as pltpu

F32 = jnp.float32
BF16 = jnp.bfloat16

N_RG_HEADS = 8
CONV_W = 4
LRU_C = 8.0
N_HEADS = 4
QK_NOPE = 128
QK_ROPE = 64
V_DIM = 128
ROPE_THETA = 10000.0
ATT_SCALE = (QK_NOPE + QK_ROPE) ** -0.5
EPS = 1e-6
MASK_VALUE = -1e30

VMEM_LIMIT_BYTES = 48 * 1024 * 1024

ROW_TILE = 512
LRU_CHUNK = 512
ATT_TQ = 512
LANE = 128


def _params(*sem):
    return pltpu.CompilerParams(dimension_semantics=sem, vmem_limit_bytes=VMEM_LIMIT_BYTES)


def _rms(v, g):
    return v * lax.rsqrt(jnp.mean(v * v, axis=-1, keepdims=True) + EPS) * g


def _silu(v):
    return v * jax.nn.sigmoid(v)


def _dot(a, b):
    return jnp.dot(a, b, preferred_element_type=F32)


def _dot_nt(a, b):
    return lax.dot_general(a, b, (((1,), (1,)), ((), ())), preferred_element_type=F32)


def _proj_kernel(x_ref, ng_ref, win_ref, qng_ref, wuq_ref, kvng_ref, wukt_ref, cos_ref, sin_ref,
                 xr_ref, sgr_ref, sga_ref, q_ref, k_ref, ckv_ref, kpe_ref, *, d_rg, q_rank, kv_rank):
    u = _rms(x_ref[...], ng_ref[...])
    z = _dot(u.astype(BF16), win_ref[...])
    o = 0
    xr_ref[...] = z[:, o:o + d_rg]; o += d_rg
    sgr_ref[...] = _silu(z[:, o:o + d_rg]); o += d_rg
    c_q = z[:, o:o + q_rank]; o += q_rank
    c_kv = z[:, o:o + kv_rank]; o += kv_rank
    d_att = N_HEADS * V_DIM
    sga_ref[...] = _silu(z[:, o:o + d_att]); o += d_att
    kpe_raw = z[:, o:o + QK_ROPE]; o += QK_ROPE
    kpe_swp = z[:, o:o + QK_ROPE]

    cos = cos_ref[...]
    sin = sin_ref[...]

    ckv_n = _rms(c_kv, kvng_ref[...])
    kpe = kpe_raw * cos[:, :QK_ROPE] + kpe_swp * sin[:, :QK_ROPE]
    ckv_ref[...] = ckv_n
    kpe_ref[...] = kpe
    rope_pad = jnp.zeros((x_ref.shape[0], LANE - QK_ROPE), BF16)
    k_ref[:, 0:kv_rank] = ckv_n.astype(BF16)
    k_ref[:, kv_rank:] = jnp.concatenate([kpe.astype(BF16), rope_pad], axis=-1)

    q = _dot(_rms(c_q, qng_ref[...]).astype(BF16), wuq_ref[...])
    pe0 = N_HEADS * QK_NOPE
    sw0 = pe0 + N_HEADS * QK_ROPE
    for pair in range(N_HEADS // 2):
        pe = q[:, pe0 + 128 * pair:pe0 + 128 * (pair + 1)]
        sw = q[:, sw0 + 128 * pair:sw0 + 128 * (pair + 1)]
        pe_r = (pe * cos + sw * sin) * ATT_SCALE
        for sub in range(2):
            h = 2 * pair + sub
            q_ref[h, :, kv_rank:] = jnp.concatenate(
                [pe_r[:, QK_ROPE * sub:QK_ROPE * (sub + 1)].astype(BF16), rope_pad], axis=-1)
    for h in range(N_HEADS):
        q_lat = _dot(q[:, QK_NOPE * h:QK_NOPE * (h + 1)].astype(BF16), wukt_ref[h])
        q_ref[h, :, 0:kv_rank] = (q_lat * ATT_SCALE).astype(BF16)


def _proj(x, lw, cos_tab, sin_tab, tab_blocks):
    n, d_model = x.shape
    tm = min(ROW_TILE, n)
    d_in = lw["w_in"].shape[1]
    d_rg = lw["conv_b"].shape[-1]
    q_rank = lw["q_norm_g"].shape[-1]
    kv_rank = lw["kv_norm_g"].shape[-1]
    d_att = N_HEADS * V_DIM
    d_qk = kv_rank + LANE
    const = lambda *shape: pl.BlockSpec(shape, lambda i: (0,) * len(shape))
    rows = lambda w: pl.BlockSpec((tm, w), lambda i: (i, 0))
    tab = pl.BlockSpec((tm, 128), lambda i: (i % tab_blocks, 0))
    kern = functools.partial(_proj_kernel, d_rg=d_rg, q_rank=q_rank, kv_rank=kv_rank)
    return pl.pallas_call(
        kern,
        grid=(n // tm,),
        in_specs=[rows(d_model), const(1, d_model), const(d_model, d_in), const(1, q_rank),
                  const(q_rank, lw["w_uq"].shape[1]), const(1, kv_rank),
                  const(N_HEADS, QK_NOPE, kv_rank), tab, tab],
        out_specs=[rows(d_rg), rows(d_rg), rows(d_att),
                   pl.BlockSpec((N_HEADS, tm, d_qk), lambda i: (0, i, 0)),
                   rows(d_qk), rows(kv_rank), rows(QK_ROPE)],
        out_shape=[jax.ShapeDtypeStruct((n, d_rg), F32), jax.ShapeDtypeStruct((n, d_rg), F32),
                   jax.ShapeDtypeStruct((n, d_att), F32),
                   jax.ShapeDtypeStruct((N_HEADS, n, d_qk), BF16),
                   jax.ShapeDtypeStruct((n, d_qk), BF16),
                   jax.ShapeDtypeStruct((n, kv_rank), F32), jax.ShapeDtypeStruct((n, QK_ROPE), F32)],
        compiler_params=_params("arbitrary"),
        name="proj",
    )(x, lw["norm_g"], lw["w_in"], lw["q_norm_g"], lw["w_uq"], lw["kv_norm_g"], lw["w_ukt"],
      cos_tab, sin_tab)


def _lru_gates(xc, wg_ref, ba_ref, bx_ref, lam_ref):
    d_rg = xc.shape[-1]
    g = _dot(xc.astype(BF16), wg_ref[...])
    r = jax.nn.sigmoid(g[:, :d_rg] + ba_ref[...])
    i = jax.nn.sigmoid(g[:, d_rg:] + bx_ref[...])
    neg_lam = -lam_ref[...]
    softplus = jnp.maximum(neg_lam, 0.0) + jnp.log1p(jnp.exp(-jnp.abs(neg_lam)))
    log_a = -LRU_C * r * softplus
    a = jnp.exp(log_a)
    return a, jnp.sqrt(-jnp.tanh(log_a) * (a * a + 1.0)) * i * xc


def _rglru_prompt_kernel(xr_ref, sgr_ref, cw_ref, cb_ref, wg_ref, ba_ref, bx_ref, lam_ref,
                         y_ref, hlast_ref, conv_ref, xpad_scr, a_scr, b_scr, h_scr, *, tc):
    ci = pl.program_id(1)
    d_rg = xr_ref.shape[-1]

    @pl.when(ci == 0)
    def _():
        xpad_scr[0:8, :] = jnp.zeros((8, d_rg), F32)
        h_scr[...] = jnp.zeros((8, d_rg), F32)

    xpad_scr[8:8 + tc, :] = xr_ref[...]
    xc = cb_ref[...] + cw_ref[0:1, :] * xpad_scr[5:5 + tc, :]
    for k in range(1, CONV_W):
        xc = xc + cw_ref[k:k + 1, :] * xpad_scr[5 + k:5 + k + tc, :]
    a, b = _lru_gates(xc, wg_ref, ba_ref, bx_ref, lam_ref)

    a = a.reshape(tc // 8, 8, d_rg)
    b = b.reshape(tc // 8, 8, d_rg)
    step = lax.broadcasted_iota(jnp.int32, a.shape, 1)
    for d in (1, 2, 4):
        keep = step >= d
        b = jnp.where(keep, a * pltpu.roll(b, d, 1) + b, b)
        a = jnp.where(keep, a * pltpu.roll(a, d, 1), a)
    a_scr[...] = a.reshape(tc, d_rg)
    b_scr[...] = b.reshape(tc, d_rg)

    def group(g, h_in):
        r0 = pl.multiple_of(g * 8, 8)
        hs = a_scr[pl.ds(r0, 8), :] * h_in + b_scr[pl.ds(r0, 8), :]
        b_scr[pl.ds(r0, 8), :] = hs
        return jnp.broadcast_to(hs[7:8, :], hs.shape)

    h_in = lax.fori_loop(0, tc // 8, group, h_scr[...], unroll=8)
    h_scr[...] = h_in
    y_ref[...] = (b_scr[...] * sgr_ref[...]).astype(y_ref.dtype)
    xpad_scr[0:8, :] = xpad_scr[tc:tc + 8, :]

    @pl.when(ci == pl.num_programs(1) - 1)
    def _():
        hlast_ref[...] = h_in[0:1, :]
        conv_ref[...] = xpad_scr[tc + 8 - (CONV_W - 1):tc + 8, :]


def _rglru_prompt(xr, sgr, lw, bsz, t):
    d_rg = xr.shape[-1]
    tc = min(LRU_CHUNK, t)
    nc = t // tc
    const = lambda *shape: pl.BlockSpec(shape, lambda b, c: (0,) * len(shape))
    rows = pl.BlockSpec((tc, d_rg), lambda b, c: (b * nc + c, 0))
    y, hlast, conv = pl.pallas_call(
        functools.partial(_rglru_prompt_kernel, tc=tc),
        grid=(bsz, nc),
        in_specs=[rows, rows, const(CONV_W, d_rg), const(1, d_rg), const(d_rg, 2 * d_rg),
                  const(1, d_rg), const(1, d_rg), const(1, d_rg)],
        out_specs=[rows, pl.BlockSpec((None, 1, d_rg), lambda b, c: (b, 0, 0)),
                   pl.BlockSpec((None, CONV_W - 1, d_rg), lambda b, c: (b, 0, 0))],
        out_shape=[jax.ShapeDtypeStruct((bsz * t, d_rg), BF16),
                   jax.ShapeDtypeStruct((bsz, 1, d_rg), F32),
                   jax.ShapeDtypeStruct((bsz, CONV_W - 1, d_rg), F32)],
        scratch_shapes=[pltpu.VMEM((tc + 8, d_rg), F32), pltpu.VMEM((tc, d_rg), F32),
                        pltpu.VMEM((tc, d_rg), F32), pltpu.VMEM((8, d_rg), F32)],
        compiler_params=_params("arbitrary", "arbitrary"),
        name="rglru_prompt",
    )(xr, sgr, lw["conv_w"], lw["conv_b"], lw["w_gate"], lw["b_a"], lw["b_x"], lw["lru_lambda"])
    return y, hlast.reshape(bsz, d_rg), conv


def _rglru_sample_kernel(xr_ref, sgr_ref, buf_ref, h0_ref, cw_ref, cb_ref, wg_ref, ba_ref, bx_ref,
                         lam_ref, y_ref, hlast_ref, conv_ref):
    t_len = xr_ref.shape[0]
    xp = [buf_ref[k] for k in range(CONV_W - 1)] + [xr_ref[t] for t in range(t_len)]
    h = h0_ref[...]
    for t in range(t_len):
        xc = cb_ref[...] + cw_ref[0:1, :] * xp[t]
        for k in range(1, CONV_W):
            xc = xc + cw_ref[k:k + 1, :] * xp[t + k]
        a, b = _lru_gates(xc, wg_ref, ba_ref, bx_ref, lam_ref)
        h = a * h + b
        y_ref[t] = (h * sgr_ref[t]).astype(y_ref.dtype)
    hlast_ref[...] = h
    for k in range(CONV_W - 1):
        conv_ref[k] = xp[t_len + k]


def _rglru_sample(xr_t, sgr_t, buf_t, h0, lw):
    t_len, bsz, d_rg = xr_t.shape
    return pl.pallas_call(
        _rglru_sample_kernel,
        out_shape=[jax.ShapeDtypeStruct((t_len, bsz, d_rg), BF16),
                   jax.ShapeDtypeStruct((bsz, d_rg), F32),
                   jax.ShapeDtypeStruct((CONV_W - 1, bsz, d_rg), F32)],
        compiler_params=pltpu.CompilerParams(vmem_limit_bytes=VMEM_LIMIT_BYTES),
        name="rglru_sample",
    )(xr_t, sgr_t, buf_t, h0, lw["conv_w"], lw["conv_b"], lw["w_gate"], lw["b_a"], lw["b_x"],
      lw["lru_lambda"])


def _attn_prompt_kernel(q_ref, k_ref, o_ref, qt_scr, vt_scr, s_scr, m_scr, l_scr, acc_scr, *, tq, kv_rank):
    qi = pl.program_id(1)
    t = k_ref.shape[0]

    @pl.when(qi == 0)
    def _():
        for c in range(t // tq):
            blk = k_ref[c * tq:(c + 1) * tq, 0:kv_rank]
            vt_scr[:, c * tq:(c + 1) * tq] = blk.astype(F32).T.astype(BF16)

    for h in range(N_HEADS):
        qt_scr[:, h * tq:(h + 1) * tq] = q_ref[h].astype(F32).T.astype(BF16)
    m_scr[...] = jnp.full(m_scr.shape, MASK_VALUE, F32)
    l_scr[...] = jnp.zeros(l_scr.shape, F32)
    acc_scr[...] = jnp.zeros(acc_scr.shape, F32)

    def scores(ki):
        kb = k_ref[pl.ds(pl.multiple_of(ki * tq, tq), tq), :]
        return _dot(kb, qt_scr[...])

    def update(ki, s):
        m_prev = m_scr[...]
        m_new = jnp.maximum(m_prev, jnp.max(s, axis=0, keepdims=True))
        alpha = jnp.exp(m_prev - m_new)
        p = jnp.exp(s - m_new)
        l_scr[...] = alpha * l_scr[...] + jnp.sum(p, axis=0, keepdims=True)
        vt = vt_scr[:, pl.ds(pl.multiple_of(ki * tq, tq), tq)]
        acc_scr[...] = alpha * acc_scr[...] + _dot(vt, p.astype(BF16))
        m_scr[...] = m_new

    s_scr[...] = scores(0)

    def body(ki, c):
        s_next = scores(ki + 1)
        update(ki, s_scr[...])
        s_scr[...] = s_next
        return c

    lax.fori_loop(0, qi, body, 0)
    s = s_scr[...]
    k_off = lax.broadcasted_iota(jnp.int32, s.shape, 0)
    q_off = lax.broadcasted_iota(jnp.int32, s.shape, 1) & (tq - 1)
    update(qi, jnp.where(k_off <= q_off, s, MASK_VALUE))

    o_t = acc_scr[...] / l_scr[...]
    for h in range(N_HEADS):
        o_ref[h] = o_t[:, h * tq:(h + 1) * tq].T.astype(o_ref.dtype)


def _attn_prompt(q, k, bsz, t, kv_rank):
    d_qk = q.shape[-1]
    tq = min(ATT_TQ, t)
    assert tq & (tq - 1) == 0 and t % tq == 0
    nq = t // tq
    cols = N_HEADS * tq
    return pl.pallas_call(
        functools.partial(_attn_prompt_kernel, tq=tq, kv_rank=kv_rank),
        grid=(bsz, nq),
        in_specs=[pl.BlockSpec((N_HEADS, tq, d_qk), lambda b, i: (0, b * nq + i, 0)),
                  pl.BlockSpec((t, d_qk), lambda b, i: (b, 0))],
        out_specs=pl.BlockSpec((N_HEADS, tq, kv_rank), lambda b, i: (0, b * nq + i, 0)),
        out_shape=jax.ShapeDtypeStruct((N_HEADS, bsz * t, kv_rank), BF16),
        scratch_shapes=[pltpu.VMEM((d_qk, cols), BF16), pltpu.VMEM((kv_rank, t), BF16),
                        pltpu.VMEM((tq, cols), F32), pltpu.VMEM((1, cols), F32),
                        pltpu.VMEM((1, cols), F32), pltpu.VMEM((kv_rank, cols), F32)],
        compiler_params=_params("arbitrary", "arbitrary"),
        name="attn_prompt",
    )(q, k)


def _attn_sample_kernel(pt_ref, q_ref, ccur_ref, pcur_ref, kv_hbm, kr_hbm, o_ref,
                        kv_buf, kr_buf, kc_scr, kp_scr, cc_scr, pc_scr, sem,
                        *, layer, n_pages, page, t_len, kv_rank):
    b = pl.program_id(0)
    slot = lax.rem(b, 2)

    def page_copies(seq, sl, j):
        pg = pt_ref[seq, j]
        off = j * page
        return (pltpu.make_async_copy(kv_hbm.at[layer, pg], kv_buf.at[sl, pl.ds(off, page), :], sem.at[sl, 0]),
                pltpu.make_async_copy(kr_hbm.at[layer, pg], kr_buf.at[sl, :, pl.ds(off, page)], sem.at[sl, 1]))

    def fetch(seq, sl):
        for j in range(n_pages):
            for cp in page_copies(seq, sl, j):
                cp.start()

    def wait_all(sl):
        pltpu.make_async_copy(kv_buf.at[sl], kv_buf.at[sl], sem.at[sl, 0]).wait()
        pltpu.make_async_copy(kr_buf.at[sl], kr_buf.at[sl], sem.at[sl, 1]).wait()

    @pl.when(b == 0)
    def _():
        fetch(0, 0)

    wait_all(slot)

    @pl.when(b + 1 < pl.num_programs(0))
    def _():
        fetch(b + 1, 1 - slot)

    kc_scr[...] = kv_buf[slot].astype(BF16)
    kp_scr[...] = kr_buf[slot].astype(BF16)
    q = q_ref[...]
    q_lat = q[:, :kv_rank]
    q_pe = q[:, kv_rank:kv_rank + QK_ROPE]
    half = (n_pages // 2) * page
    halves = (slice(0, half), slice(half, n_pages * page))
    s = [_dot_nt(q_lat, kc_scr[r, :]) + _dot(q_pe, kp_scr[:, r]) for r in halves]

    cc_scr[...] = jnp.zeros(cc_scr.shape, F32)
    pc_scr[...] = jnp.zeros(pc_scr.shape, F32)
    cc_scr[0:t_len, :] = ccur_ref[...]
    pc_scr[0:t_len, :] = pcur_ref[...]
    vc = cc_scr[...].astype(BF16)
    s_cur = _dot_nt(q_lat, vc) + _dot_nt(q_pe, pc_scr[...].astype(BF16))
    q_step = lax.rem(lax.broadcasted_iota(jnp.int32, s_cur.shape, 0), t_len)
    k_step = lax.broadcasted_iota(jnp.int32, s_cur.shape, 1)
    s_cur = jnp.where(k_step <= q_step, s_cur, MASK_VALUE)

    parts = s + [s_cur]
    m = functools.reduce(jnp.maximum, [jnp.max(v, axis=-1, keepdims=True) for v in parts])
    p = [jnp.exp(v - m) for v in parts]
    l = sum(jnp.sum(v, axis=-1, keepdims=True) for v in p)
    o = _dot(p[2].astype(BF16), vc)
    for v, r in zip(p[:2], halves):
        o = o + _dot(v.astype(BF16), kc_scr[r, :])
    o_ref[...] = (o / l).astype(o_ref.dtype)


def _attn_sample(q, ckv, kpe, cache_kv, cache_kr_t, page_table, layer):
    bsz, rows, d_qk = q.shape
    t_len = ckv.shape[1]
    kv_rank = ckv.shape[-1]
    page = cache_kv.shape[2]
    n_pages = page_table.shape[1]
    past = n_pages * page
    kern = functools.partial(_attn_sample_kernel, layer=layer, n_pages=n_pages, page=page,
                             t_len=t_len, kv_rank=kv_rank)
    return pl.pallas_call(
        kern,
        grid_spec=pltpu.PrefetchScalarGridSpec(
            num_scalar_prefetch=1,
            grid=(bsz,),
            in_specs=[pl.BlockSpec((None, rows, d_qk), lambda b, pt: (b, 0, 0)),
                      pl.BlockSpec((None, t_len, kv_rank), lambda b, pt: (b, 0, 0)),
                      pl.BlockSpec((None, t_len, QK_ROPE), lambda b, pt: (b, 0, 0)),
                      pl.BlockSpec(memory_space=pl.ANY), pl.BlockSpec(memory_space=pl.ANY)],
            out_specs=pl.BlockSpec((None, rows, kv_rank), lambda b, pt: (b, 0, 0)),
            scratch_shapes=[pltpu.VMEM((2, past, kv_rank), F32), pltpu.VMEM((2, QK_ROPE, past), F32),
                            pltpu.VMEM((past, kv_rank), BF16), pltpu.VMEM((QK_ROPE, past), BF16),
                            pltpu.VMEM((page, kv_rank), F32), pltpu.VMEM((page, QK_ROPE), F32),
                            pltpu.SemaphoreType.DMA((2, 2))]),
        out_shape=jax.ShapeDtypeStruct((bsz, rows, kv_rank), BF16),
        compiler_params=_params("arbitrary"),
        name="attn_sample",
    )(page_table, q, ckv, kpe, cache_kv, cache_kr_t)


def _out_kernel(x_ref, yr_ref, o_ref, sga_ref, wuv_ref, wout_ref, fg_ref, out_ref, *, final):
    d_rg = yr_ref.shape[-1]
    sga = sga_ref[...]
    acc = x_ref[...] + _dot(yr_ref[...], wout_ref[0:d_rg, :])
    y_a = [(_dot(o_ref[h], wuv_ref[h]) * sga[:, V_DIM * h:V_DIM * (h + 1)]).astype(BF16)
           for h in range(N_HEADS)]
    acc = acc + _dot(jnp.concatenate(y_a, axis=-1), wout_ref[d_rg:, :])
    out_ref[...] = _rms(acc, fg_ref[...]) if final else acc


def _out(x, yr, o_lat, sga, lw, final_g, final):
    n, d_model = x.shape
    tm = min(ROW_TILE, n)
    d_rg = yr.shape[-1]
    kv_rank = o_lat.shape[-1]
    d_att = sga.shape[-1]
    const = lambda *shape: pl.BlockSpec(shape, lambda i: (0,) * len(shape))
    rows = lambda w: pl.BlockSpec((tm, w), lambda i: (i, 0))
    return pl.pallas_call(
        functools.partial(_out_kernel, final=final),
        grid=(n // tm,),
        in_specs=[rows(d_model), rows(d_rg), pl.BlockSpec((N_HEADS, tm, kv_rank), lambda i: (0, i, 0)),
                  rows(d_att), const(N_HEADS, kv_rank, V_DIM), const(d_rg + d_att, d_model),
                  const(1, d_model)],
        out_specs=rows(d_model),
        out_shape=jax.ShapeDtypeStruct((n, d_model), F32),
        compiler_params=_params("arbitrary"),
        name="out",
    )(x, yr, o_lat, sga, lw["w_uv"], lw["w_out"], final_g)


def _rope_tables(pos):
    half = QK_ROPE // 2
    freqs = ROPE_THETA ** (-jnp.arange(half, dtype=F32) / half)
    ang = pos.astype(F32)[:, None] * freqs
    cos, sin = jnp.cos(ang), jnp.sin(ang)
    return jnp.concatenate([cos, cos, cos, cos], axis=-1), jnp.concatenate([-sin, sin, -sin, sin], axis=-1)


def _swap_halves(w):
    half = w.shape[-1] // 2
    return jnp.concatenate([w[..., half:], w[..., :half]], axis=-1)


def _prepare_weights(norm_g, w_in, conv_w, conv_b, w_a, b_a, w_x, b_x, lru_lambda, q_norm_g, w_uq,
                     kv_norm_g, w_uk, w_uv, w_out):
    depth, d_model, _ = w_in.shape
    d_rg = conv_b.shape[-1]
    q_rank = q_norm_g.shape[-1]
    kv_rank = kv_norm_g.shape[-1]
    s0, s1, s2, s3 = d_rg, 2 * d_rg, 2 * d_rg + q_rank, 2 * d_rg + q_rank + kv_rank
    s4 = s3 + QK_ROPE
    k_pe_cols = w_in[..., s3:s4]
    w_in2 = jnp.concatenate([w_in[..., :s3], w_in[..., s4:], k_pe_cols, _swap_halves(k_pe_cols)], axis=-1)
    wq = w_uq.reshape(depth, q_rank, N_HEADS, QK_NOPE + QK_ROPE)
    q_pe_cols = wq[..., QK_NOPE:]
    w_uq2 = jnp.concatenate([wq[..., :QK_NOPE].reshape(depth, q_rank, -1),
                             q_pe_cols.reshape(depth, q_rank, -1),
                             _swap_halves(q_pe_cols).reshape(depth, q_rank, -1)], axis=-1)
    eye = jnp.eye(N_RG_HEADS, dtype=w_a.dtype)
    block_diag = lambda w: jnp.einsum("lhij,hg->lhigj", w, eye).reshape(depth, d_rg, d_rg)
    w_gate = jnp.concatenate([block_diag(w_a), block_diag(w_x)], axis=-1)
    row = lambda v: v[:, None, :]
    return dict(
        norm_g=row(norm_g), w_in=w_in2.astype(BF16), conv_w=conv_w, conv_b=row(conv_b),
        w_gate=w_gate.astype(BF16), b_a=row(b_a), b_x=row(b_x), lru_lambda=row(lru_lambda),
        q_norm_g=row(q_norm_g), w_uq=w_uq2.astype(BF16), kv_norm_g=row(kv_norm_g),
        w_ukt=jnp.transpose(w_uk, (0, 2, 3, 1)).astype(BF16),
        w_uv=jnp.transpose(w_uv, (0, 2, 1, 3)).astype(BF16),
        w_out=w_out.astype(BF16))


def kernel(x_prompt, x_sample, cache_kv_latent, cache_k_rope, page_table, state_rglru, state_conv,
           norm_g, w_in, conv_w, conv_b, w_a, b_a, w_x, b_x, lru_lambda, q_norm_g, w_uq,
           kv_norm_g, w_uk, w_uv, w_out, final_norm_g):
    bp, t, d_model = x_prompt.shape
    bs, ts, _ = x_sample.shape
    depth = w_in.shape[0]
    kv_rank = kv_norm_g.shape[-1]
    d_rg = conv_b.shape[-1]
    past_len = page_table.shape[1] * cache_kv_latent.shape[2]

    weights = _prepare_weights(norm_g, w_in, conv_w, conv_b, w_a, b_a, w_x, b_x, lru_lambda,
                               q_norm_g, w_uq, kv_norm_g, w_uk, w_uv, w_out)
    final_g = final_norm_g[None, :]
    cos_p, sin_p = _rope_tables(jnp.arange(t, dtype=jnp.int32))
    pos_s = past_len + jnp.arange(ts, dtype=jnp.int32)
    cos_s, sin_s = _rope_tables(jnp.tile(pos_s, bs))
    tab_blocks_p = t // min(ROW_TILE, bp * t)
    tab_blocks_s = (bs * ts) // min(ROW_TILE, bs * ts)
    buf_t = jnp.transpose(state_conv, (0, 2, 1, 3))
    cache_kr_t = jnp.swapaxes(cache_k_rope, 2, 3)

    x_p = x_prompt.reshape(bp * t, d_model)
    x_s = x_sample.reshape(bs * ts, d_model)
    kv_p, kr_p, h_p, cv_p = [], [], [], []
    kv_s, kr_s, h_s, cv_s = [], [], [], []
    for l in range(depth):
        lw = {name: w[l] for name, w in weights.items()}
        final = l == depth - 1

        xr, sgr, sga, q, k, ckv, kpe = _proj(x_p, lw, cos_p, sin_p, tab_blocks_p)
        yr, hl, cv = _rglru_prompt(xr, sgr, lw, bp, t)
        o_lat = _attn_prompt(q, k, bp, t, kv_rank)
        x_p = _out(x_p, yr, o_lat, sga, lw, final_g, final)
        kv_p.append(ckv.reshape(bp, t, kv_rank)); kr_p.append(kpe.reshape(bp, t, QK_ROPE))
        h_p.append(hl); cv_p.append(cv)

        xr, sgr, sga, q, _, ckv, kpe = _proj(x_s, lw, cos_s, sin_s, tab_blocks_s)
        to_time_major = lambda v: jnp.transpose(v.reshape(bs, ts, -1), (1, 0, 2))
        yr_t, hl, cv_t = _rglru_sample(to_time_major(xr), to_time_major(sgr), buf_t[l],
                                       state_rglru[l], lw)
        yr = jnp.transpose(yr_t, (1, 0, 2)).reshape(bs * ts, d_rg)
        ckv = ckv.reshape(bs, ts, kv_rank)
        kpe = kpe.reshape(bs, ts, QK_ROPE)
        q_b = jnp.transpose(q.reshape(N_HEADS, bs, ts, -1), (1, 0, 2, 3)).reshape(bs, N_HEADS * ts, -1)
        o_b = _attn_sample(q_b, ckv, kpe, cache_kv_latent, cache_kr_t, page_table, l)
        o_lat = jnp.transpose(o_b.reshape(bs, N_HEADS, ts, kv_rank), (1, 0, 2, 3)).reshape(
            N_HEADS, bs * ts, kv_rank)
        x_s = _out(x_s, yr, o_lat, sga, lw, final_g, final)
        kv_s.append(ckv); kr_s.append(kpe); h_s.append(hl); cv_s.append(cv_t)

    y_prompt = x_p.reshape(bp, t, d_model)
    y_sample = x_s.reshape(bs, ts, d_model)
    conv_sample = jnp.transpose(jnp.stack(cv_s), (0, 2, 1, 3))
    return (y_prompt, y_sample,
            jnp.stack(kv_p), jnp.stack(kr_p), jnp.stack(h_p), jnp.stack(cv_p),
            jnp.stack(kv_s), jnp.stack(kr_s), jnp.stack(h_s), conv_sample)
```

```python
import functools

import jax
import jax.numpy as jnp
from jax import lax
from jax.experimental import pallas as pl
from jax.experimental.pallas import tpu as pltpu

F32 = jnp.float32
BF16 = jnp.bfloat16

N_RG_HEADS = 8
CONV_W = 4
LRU_C = 8.0
N_HEADS = 4
QK_NOPE = 128
QK_ROPE = 64
V_DIM = 128
ROPE_THETA = 10000.0
ATT_SCALE = (QK_NOPE + QK_ROPE) ** -0.5
EPS = 1e-6
MASK_VALUE = -1e30

VMEM_LIMIT_BYTES = 48 * 1024 * 1024

ROW_TILE = 512
LRU_CHUNK = 512
ATT_TQ = 512
SAMPLE_SECTIONS = 2
LANE = 128


def _params(*sem):
    return pltpu.CompilerParams(dimension_semantics=sem, vmem_limit_bytes=VMEM_LIMIT_BYTES)


def _layer_spec(w, layer):
    zeros = (0,) * (w.ndim - 1)
    return pl.BlockSpec((None,) + w.shape[1:], lambda *grid: (layer,) + zeros)


def _rms(v, g):
    return v * lax.rsqrt(jnp.mean(v * v, axis=-1, keepdims=True) + EPS) * g


def _silu(v):
    return v * jax.nn.sigmoid(v)


def _dot(a, b):
    return jnp.dot(a, b, preferred_element_type=F32)


def _dot_nt(a, b):
    return lax.dot_general(a, b, (((1,), (1,)), ((), ())), preferred_element_type=F32)


def _proj_kernel(x_ref, ng_ref, win_ref, qng_ref, wuq_ref, kvng_ref, wukt_ref, cos_ref, sin_ref,
                 xr_ref, sgr_ref, sga_ref, q_ref, k_ref, ckv_ref, kpe_ref, *, d_rg, q_rank, kv_rank):
    u = _rms(x_ref[...], ng_ref[...])
    z = _dot(u.astype(BF16), win_ref[...])
    o = 0
    xr_ref[...] = z[:, o:o + d_rg]; o += d_rg
    sgr_ref[...] = _silu(z[:, o:o + d_rg]); o += d_rg
    c_q = z[:, o:o + q_rank]; o += q_rank
    c_kv = z[:, o:o + kv_rank]; o += kv_rank
    d_att = N_HEADS * V_DIM
    sga_ref[...] = _silu(z[:, o:o + d_att]); o += d_att
    kpe_raw = z[:, o:o + QK_ROPE]; o += QK_ROPE
    kpe_swp = z[:, o:o + QK_ROPE]

    cos = cos_ref[...]
    sin = sin_ref[...]

    ckv_n = _rms(c_kv, kvng_ref[...])
    kpe = kpe_raw * cos[:, :QK_ROPE] + kpe_swp * sin[:, :QK_ROPE]
    ckv_ref[...] = ckv_n
    kpe_ref[...] = kpe
    rope_pad = jnp.zeros((x_ref.shape[0], LANE - QK_ROPE), BF16)
    k_ref[:, 0:kv_rank] = ckv_n.astype(BF16)
    k_ref[:, kv_rank:] = jnp.concatenate([kpe.astype(BF16), rope_pad], axis=-1)

    q = _dot(_rms(c_q, qng_ref[...]).astype(BF16), wuq_ref[...])
    pe0 = N_HEADS * QK_NOPE
    sw0 = pe0 + N_HEADS * QK_ROPE
    for pair in range(N_HEADS // 2):
        pe = q[:, pe0 + 128 * pair:pe0 + 128 * (pair + 1)]
        sw = q[:, sw0 + 128 * pair:sw0 + 128 * (pair + 1)]
        pe_r = (pe * cos + sw * sin) * ATT_SCALE
        for sub in range(2):
            h = 2 * pair + sub
            q_ref[h, :, kv_rank:] = jnp.concatenate(
                [pe_r[:, QK_ROPE * sub:QK_ROPE * (sub + 1)].astype(BF16), rope_pad], axis=-1)
    for h in range(N_HEADS):
        q_lat = _dot(q[:, QK_NOPE * h:QK_NOPE * (h + 1)].astype(BF16), wukt_ref[h])
        q_ref[h, :, 0:kv_rank] = (q_lat * ATT_SCALE).astype(BF16)


def _proj(x, ws, layer, cos_tab, sin_tab, tab_blocks):
    n, d_model = x.shape
    tm = min(ROW_TILE, n)
    d_rg = ws["conv_b"].shape[-1]
    q_rank = ws["q_norm_g"].shape[-1]
    kv_rank = ws["kv_norm_g"].shape[-1]
    d_att = N_HEADS * V_DIM
    d_qk = kv_rank + LANE
    rows = lambda w: pl.BlockSpec((tm, w), lambda i: (i, 0))
    tab = pl.BlockSpec((tm, LANE), lambda i: (i % tab_blocks, 0))
    names = ("norm_g", "w_in", "q_norm_g", "w_uq", "kv_norm_g", "w_ukt")
    kern = functools.partial(_proj_kernel, d_rg=d_rg, q_rank=q_rank, kv_rank=kv_rank)
    return pl.pallas_call(
        kern,
        grid=(n // tm,),
        in_specs=[rows(d_model)] + [_layer_spec(ws[k], layer) for k in names] + [tab, tab],
        out_specs=[rows(d_rg), rows(d_rg), rows(d_att),
                   pl.BlockSpec((N_HEADS, tm, d_qk), lambda i: (0, i, 0)),
                   rows(d_qk), rows(kv_rank), rows(QK_ROPE)],
        out_shape=[jax.ShapeDtypeStruct((n, d_rg), F32), jax.ShapeDtypeStruct((n, d_rg), F32),
                   jax.ShapeDtypeStruct((n, d_att), F32),
                   jax.ShapeDtypeStruct((N_HEADS, n, d_qk), BF16),
                   jax.ShapeDtypeStruct((n, d_qk), BF16),
                   jax.ShapeDtypeStruct((n, kv_rank), F32), jax.ShapeDtypeStruct((n, QK_ROPE), F32)],
        compiler_params=_params("arbitrary"),
        name="proj",
    )(x, *[ws[k] for k in names], cos_tab, sin_tab)


LRU_WEIGHTS = ("conv_w", "conv_b", "w_gate", "b_a", "b_x", "lru_lambda")


def _lru_gates(xc, wg_ref, ba_ref, bx_ref, lam_ref):
    d_rg = xc.shape[-1]
    g = _dot(xc.astype(BF16), wg_ref[...])
    r = jax.nn.sigmoid(g[:, :d_rg] + ba_ref[...])
    i = jax.nn.sigmoid(g[:, d_rg:] + bx_ref[...])
    neg_lam = -lam_ref[...]
    softplus = jnp.maximum(neg_lam, 0.0) + jnp.log1p(jnp.exp(-jnp.abs(neg_lam)))
    log_a = -LRU_C * r * softplus
    a = jnp.exp(log_a)
    return a, jnp.sqrt(-jnp.tanh(log_a) * (a * a + 1.0)) * i * xc


def _rglru_prompt_kernel(xr_ref, sgr_ref, cw_ref, cb_ref, wg_ref, ba_ref, bx_ref, lam_ref,
                         y_ref, hlast_ref, conv_ref, xpad_scr, a_scr, b_scr, h_scr, *, tc):
    ci = pl.program_id(1)
    d_rg = xr_ref.shape[-1]

    @pl.when(ci == 0)
    def _():
        xpad_scr[0:8, :] = jnp.zeros((8, d_rg), F32)
        h_scr[...] = jnp.zeros((8, d_rg), F32)

    xpad_scr[8:8 + tc, :] = xr_ref[...]
    xc = cb_ref[...] + cw_ref[0:1, :] * xpad_scr[5:5 + tc, :]
    for k in range(1, CONV_W):
        xc = xc + cw_ref[k:k + 1, :] * xpad_scr[5 + k:5 + k + tc, :]
    a, b = _lru_gates(xc, wg_ref, ba_ref, bx_ref, lam_ref)

    a = a.reshape(tc // 8, 8, d_rg)
    b = b.reshape(tc // 8, 8, d_rg)
    step = lax.broadcasted_iota(jnp.int32, a.shape, 1)
    for d in (1, 2, 4):
        keep = step >= d
        b = jnp.where(keep, a * pltpu.roll(b, d, 1) + b, b)
        a = jnp.where(keep, a * pltpu.roll(a, d, 1), a)
    a_scr[...] = a.reshape(tc, d_rg)
    b_scr[...] = b.reshape(tc, d_rg)

    def group(g, h_in):
        r0 = pl.multiple_of(g * 8, 8)
        hs = a_scr[pl.ds(r0, 8), :] * h_in + b_scr[pl.ds(r0, 8), :]
        b_scr[pl.ds(r0, 8), :] = hs
        return jnp.broadcast_to(hs[7:8, :], hs.shape)

    h_in = lax.fori_loop(0, tc // 8, group, h_scr[...], unroll=8)
    h_scr[...] = h_in
    y_ref[...] = (b_scr[...] * sgr_ref[...]).astype(y_ref.dtype)
    xpad_scr[0:8, :] = xpad_scr[tc:tc + 8, :]

    @pl.when(ci == pl.num_programs(1) - 1)
    def _():
        hlast_ref[...] = h_in[0:1, :]
        conv_ref[...] = xpad_scr[tc + 8 - (CONV_W - 1):tc + 8, :]


def _rglru_prompt(xr, sgr, ws, layer, bsz, t):
    d_rg = xr.shape[-1]
    tc = min(LRU_CHUNK, t)
    nc = t // tc
    rows = pl.BlockSpec((tc, d_rg), lambda b, c: (b * nc + c, 0))
    y, hlast, conv = pl.pallas_call(
        functools.partial(_rglru_prompt_kernel, tc=tc),
        grid=(bsz, nc),
        in_specs=[rows, rows] + [_layer_spec(ws[k], layer) for k in LRU_WEIGHTS],
        out_specs=[rows, pl.BlockSpec((None, 1, d_rg), lambda b, c: (b, 0, 0)),
                   pl.BlockSpec((None, CONV_W - 1, d_rg), lambda b, c: (b, 0, 0))],
        out_shape=[jax.ShapeDtypeStruct((bsz * t, d_rg), BF16),
                   jax.ShapeDtypeStruct((bsz, 1, d_rg), F32),
                   jax.ShapeDtypeStruct((bsz, CONV_W - 1, d_rg), F32)],
        scratch_shapes=[pltpu.VMEM((tc + 8, d_rg), F32), pltpu.VMEM((tc, d_rg), F32),
                        pltpu.VMEM((tc, d_rg), F32), pltpu.VMEM((8, d_rg), F32)],
        compiler_params=_params("arbitrary", "arbitrary"),
        name="rglru_prompt",
    )(xr, sgr, *[ws[k] for k in LRU_WEIGHTS])
    return y, hlast.reshape(bsz, d_rg), conv


def _rglru_sample_kernel(xr_ref, sgr_ref, buf_ref, h0_ref, cw_ref, cb_ref, wg_ref, ba_ref, bx_ref,
                         lam_ref, y_ref, hlast_ref, conv_ref):
    t_len = xr_ref.shape[0]
    xp = [buf_ref[k] for k in range(CONV_W - 1)] + [xr_ref[t] for t in range(t_len)]
    h = h0_ref[...]
    for t in range(t_len):
        xc = cb_ref[...] + cw_ref[0:1, :] * xp[t]
        for k in range(1, CONV_W):
            xc = xc + cw_ref[k:k + 1, :] * xp[t + k]
        a, b = _lru_gates(xc, wg_ref, ba_ref, bx_ref, lam_ref)
        h = a * h + b
        y_ref[t] = (h * sgr_ref[t]).astype(y_ref.dtype)
    hlast_ref[...] = h
    for k in range(CONV_W - 1):
        conv_ref[k] = xp[t_len + k]


def _rglru_sample(xr_t, sgr_t, buf_t, h0, ws, layer):
    t_len, bsz, d_rg = xr_t.shape
    whole = lambda shape: pl.BlockSpec(shape, lambda i: (0,) * len(shape))
    return pl.pallas_call(
        _rglru_sample_kernel,
        grid=(1,),
        in_specs=[whole(xr_t.shape), whole(sgr_t.shape), _layer_spec(buf_t, layer), _layer_spec(h0, layer)]
        + [_layer_spec(ws[k], layer) for k in LRU_WEIGHTS],
        out_specs=[whole((t_len, bsz, d_rg)), whole((bsz, d_rg)), whole((CONV_W - 1, bsz, d_rg))],
        out_shape=[jax.ShapeDtypeStruct((t_len, bsz, d_rg), BF16),
                   jax.ShapeDtypeStruct((bsz, d_rg), F32),
                   jax.ShapeDtypeStruct((CONV_W - 1, bsz, d_rg), F32)],
        compiler_params=_params("arbitrary"),
        name="rglru_sample",
    )(xr_t, sgr_t, buf_t, h0, *[ws[k] for k in LRU_WEIGHTS])


def _attn_prompt_kernel(q_ref, k_ref, o_ref, qt_scr, vt_scr, s_scr, m_scr, l_scr, acc_scr, *, tq, kv_rank):
    qi = pl.program_id(1)
    t = k_ref.shape[0]

    @pl.when(qi == 0)
    def _():
        for c in range(t // tq):
            blk = k_ref[c * tq:(c + 1) * tq, 0:kv_rank]
            vt_scr[:, c * tq:(c + 1) * tq] = blk.astype(F32).T.astype(BF16)

    for h in range(N_HEADS):
        qt_scr[:, h * tq:(h + 1) * tq] = q_ref[h].astype(F32).T.astype(BF16)
    m_scr[...] = jnp.full(m_scr.shape, MASK_VALUE, F32)
    l_scr[...] = jnp.zeros(l_scr.shape, F32)
    acc_scr[...] = jnp.zeros(acc_scr.shape, F32)

    def scores(ki):
        kb = k_ref[pl.ds(pl.multiple_of(ki * tq, tq), tq), :]
        return _dot(kb, qt_scr[...])

    def update(ki, s):
        m_prev = m_scr[...]
        m_new = jnp.maximum(m_prev, jnp.max(s, axis=0, keepdims=True))
        alpha = jnp.exp(m_prev - m_new)
        p = jnp.exp(s - m_new)
        l_scr[...] = alpha * l_scr[...] + jnp.sum(p, axis=0, keepdims=True)
        vt = vt_scr[:, pl.ds(pl.multiple_of(ki * tq, tq), tq)]
        acc_scr[...] = alpha * acc_scr[...] + _dot(vt, p.astype(BF16))
        m_scr[...] = m_new

    s_scr[...] = scores(0)

    def body(ki, c):
        s_next = scores(ki + 1)
        update(ki, s_scr[...])
        s_scr[...] = s_next
        return c

    lax.fori_loop(0, qi, body, 0)
    s = s_scr[...]
    k_off = lax.broadcasted_iota(jnp.int32, s.shape, 0)
    q_off = lax.broadcasted_iota(jnp.int32, s.shape, 1) & (tq - 1)
    update(qi, jnp.where(k_off <= q_off, s, MASK_VALUE))

    o_t = acc_scr[...] / l_scr[...]
    for h in range(N_HEADS):
        o_ref[h] = o_t[:, h * tq:(h + 1) * tq].T.astype(o_ref.dtype)


def _attn_prompt(q, k, bsz, t, kv_rank):
    d_qk = q.shape[-1]
    tq = min(ATT_TQ, t)
    assert tq & (tq - 1) == 0 and t % tq == 0
    nq = t // tq
    cols = N_HEADS * tq
    return pl.pallas_call(
        functools.partial(_attn_prompt_kernel, tq=tq, kv_rank=kv_rank),
        grid=(bsz, nq),
        in_specs=[pl.BlockSpec((N_HEADS, tq, d_qk), lambda b, i: (0, b * nq + i, 0)),
                  pl.BlockSpec((t, d_qk), lambda b, i: (b, 0))],
        out_specs=pl.BlockSpec((N_HEADS, tq, kv_rank), lambda b, i: (0, b * nq + i, 0)),
        out_shape=jax.ShapeDtypeStruct((N_HEADS, bsz * t, kv_rank), BF16),
        scratch_shapes=[pltpu.VMEM((d_qk, cols), BF16), pltpu.VMEM((kv_rank, t), BF16),
                        pltpu.VMEM((tq, cols), F32), pltpu.VMEM((1, cols), F32),
                        pltpu.VMEM((1, cols), F32), pltpu.VMEM((kv_rank, cols), F32)],
        compiler_params=_params("arbitrary", "arbitrary"),
        name="attn_prompt",
    )(q, k)


def _attn_sample_kernel(pt_ref, q_ref, ccur_ref, pcur_ref, kv_hbm, kr_hbm, o_ref,
                        kv_buf, kr_buf, cc_scr, pc_scr, sem,
                        *, layer, n_pages, page, t_len, kv_rank):
    b = pl.program_id(0)
    slot = lax.rem(b, 2)

    def page_copies(seq, sl, j):
        pg = pt_ref[seq, j]
        off = j * page
        return (pltpu.make_async_copy(kv_hbm.at[layer, pg], kv_buf.at[sl, pl.ds(off, page), :], sem.at[sl, 0]),
                pltpu.make_async_copy(kr_hbm.at[layer, pg], kr_buf.at[sl, :, pl.ds(off, page)], sem.at[sl, 1]))

    def fetch(seq, sl):
        for j in range(n_pages):
            for cp in page_copies(seq, sl, j):
                cp.start()

    def wait_all(sl):
        pltpu.make_async_copy(kv_buf.at[sl], kv_buf.at[sl], sem.at[sl, 0]).wait()
        pltpu.make_async_copy(kr_buf.at[sl], kr_buf.at[sl], sem.at[sl, 1]).wait()

    @pl.when(b == 0)
    def _():
        fetch(0, 0)

    wait_all(slot)

    @pl.when(b + 1 < pl.num_programs(0))
    def _():
        fetch(b + 1, 1 - slot)

    q = q_ref[...]
    q_lat = q[:, :kv_rank]
    q_pe = q[:, kv_rank:kv_rank + QK_ROPE]

    def local(s, v):
        m = jnp.max(s, axis=-1, keepdims=True)
        p = jnp.exp(s - m)
        return m, jnp.sum(p, axis=-1, keepdims=True), _dot(p.astype(BF16), v)

    cc_scr[...] = jnp.zeros(cc_scr.shape, F32)
    pc_scr[...] = jnp.zeros(pc_scr.shape, F32)
    cc_scr[0:t_len, :] = ccur_ref[...]
    pc_scr[0:t_len, :] = pcur_ref[...]
    vc = cc_scr[...].astype(BF16)
    s_cur = _dot_nt(q_lat, vc) + _dot_nt(q_pe, pc_scr[...].astype(BF16))
    q_step = lax.rem(lax.broadcasted_iota(jnp.int32, s_cur.shape, 0), t_len)
    k_step = lax.broadcasted_iota(jnp.int32, s_cur.shape, 1)
    s_cur = jnp.where(k_step <= q_step, s_cur, MASK_VALUE)

    parts = [local(s_cur, vc)]
    sec = (n_pages // SAMPLE_SECTIONS) * page
    for i in range(SAMPLE_SECTIONS):
        r = slice(i * sec, (i + 1) * sec)
        kc = kv_buf[slot, r, :]
        s = _dot(q_lat, kc.T.astype(BF16)) + _dot(q_pe, kr_buf[slot, :, r].astype(BF16))
        parts.append(local(s, kc.astype(BF16)))
    m = functools.reduce(jnp.maximum, [pm for pm, _, _ in parts])
    w = [jnp.exp(pm - m) for pm, _, _ in parts]
    l = sum(wi * pl_ for wi, (_, pl_, _) in zip(w, parts))
    o = sum(wi * po for wi, (_, _, po) in zip(w, parts))
    o_ref[...] = (o / l).astype(o_ref.dtype)


def _attn_sample(q, ckv, kpe, cache_kv, cache_kr_t, page_table, layer):
    bsz, rows, d_qk = q.shape
    t_len = ckv.shape[1]
    kv_rank = ckv.shape[-1]
    page = cache_kv.shape[2]
    n_pages = page_table.shape[1]
    assert n_pages % SAMPLE_SECTIONS == 0
    past = n_pages * page
    kern = functools.partial(_attn_sample_kernel, layer=layer, n_pages=n_pages, page=page,
                             t_len=t_len, kv_rank=kv_rank)
    return pl.pallas_call(
        kern,
        grid_spec=pltpu.PrefetchScalarGridSpec(
            num_scalar_prefetch=1,
            grid=(bsz,),
            in_specs=[pl.BlockSpec((None, rows, d_qk), lambda b, pt: (b, 0, 0)),
                      pl.BlockSpec((None, t_len, kv_rank), lambda b, pt: (b, 0, 0)),
                      pl.BlockSpec((None, t_len, QK_ROPE), lambda b, pt: (b, 0, 0)),
                      pl.BlockSpec(memory_space=pl.ANY), pl.BlockSpec(memory_space=pl.ANY)],
            out_specs=pl.BlockSpec((None, rows, kv_rank), lambda b, pt: (b, 0, 0)),
            scratch_shapes=[pltpu.VMEM((2, past, kv_rank), F32), pltpu.VMEM((2, QK_ROPE, past), F32),
                            pltpu.VMEM((page, kv_rank), F32), pltpu.VMEM((page, QK_ROPE), F32),
                            pltpu.SemaphoreType.DMA((2, 2))]),
        out_shape=jax.ShapeDtypeStruct((bsz, rows, kv_rank), BF16),
        compiler_params=_params("arbitrary"),
        name="attn_sample",
    )(page_table, q, ckv, kpe, cache_kv, cache_kr_t)


def _out_kernel(x_ref, yr_ref, o_ref, sga_ref, wuv_ref, wout_ref, fg_ref, out_ref, *, final):
    d_rg = yr_ref.shape[-1]
    sga = sga_ref[...]
    acc = x_ref[...] + _dot(yr_ref[...], wout_ref[0:d_rg, :])
    y_a = [(_dot(o_ref[h], wuv_ref[h]) * sga[:, V_DIM * h:V_DIM * (h + 1)]).astype(BF16)
           for h in range(N_HEADS)]
    acc = acc + _dot(jnp.concatenate(y_a, axis=-1), wout_ref[d_rg:, :])
    out_ref[...] = _rms(acc, fg_ref[...]) if final else acc


def _out(x, yr, o_lat, sga, ws, layer, final_g, final):
    n, d_model = x.shape
    tm = min(ROW_TILE, n)
    d_rg = yr.shape[-1]
    kv_rank = o_lat.shape[-1]
    d_att = sga.shape[-1]
    rows = lambda w: pl.BlockSpec((tm, w), lambda i: (i, 0))
    return pl.pallas_call(
        functools.partial(_out_kernel, final=final),
        grid=(n // tm,),
        in_specs=[rows(d_model), rows(d_rg), pl.BlockSpec((N_HEADS, tm, kv_rank), lambda i: (0, i, 0)),
                  rows(d_att), _layer_spec(ws["w_uv"], layer), _layer_spec(ws["w_out"], layer),
                  pl.BlockSpec((1, d_model), lambda i: (0, 0))],
        out_specs=rows(d_model),
        out_shape=jax.ShapeDtypeStruct((n, d_model), F32),
        compiler_params=_params("arbitrary"),
        name="out",
    )(x, yr, o_lat, sga, ws["w_uv"], ws["w_out"], final_g)


def _rope_tables(pos):
    half = QK_ROPE // 2
    freqs = ROPE_THETA ** (-jnp.arange(half, dtype=F32) / half)
    ang = pos.astype(F32)[:, None] * freqs
    cos, sin = jnp.cos(ang), jnp.sin(ang)
    return jnp.concatenate([cos, cos, cos, cos], axis=-1), jnp.concatenate([-sin, sin, -sin, sin], axis=-1)


def _swap_halves(w):
    half = w.shape[-1] // 2
    return jnp.concatenate([w[..., half:], w[..., :half]], axis=-1)


def _prepare_weights(norm_g, w_in, conv_w, conv_b, w_a, b_a, w_x, b_x, lru_lambda, q_norm_g, w_uq,
                     kv_norm_g, w_uk, w_uv, w_out):
    depth, d_model, _ = w_in.shape
    d_rg = conv_b.shape[-1]
    q_rank = q_norm_g.shape[-1]
    kv_rank = kv_norm_g.shape[-1]
    s3 = 2 * d_rg + q_rank + kv_rank
    s4 = s3 + QK_ROPE
    k_pe_cols = w_in[..., s3:s4]
    w_in2 = jnp.concatenate([w_in[..., :s3], w_in[..., s4:], k_pe_cols, _swap_halves(k_pe_cols)], axis=-1)
    wq = w_uq.reshape(depth, q_rank, N_HEADS, QK_NOPE + QK_ROPE)
    q_pe_cols = wq[..., QK_NOPE:]
    w_uq2 = jnp.concatenate([wq[..., :QK_NOPE].reshape(depth, q_rank, -1),
                             q_pe_cols.reshape(depth, q_rank, -1),
                             _swap_halves(q_pe_cols).reshape(depth, q_rank, -1)], axis=-1)
    eye = jnp.eye(N_RG_HEADS, dtype=w_a.dtype)
    block_diag = lambda w: jnp.einsum("lhij,hg->lhigj", w, eye).reshape(depth, d_rg, d_rg)
    w_gate = jnp.concatenate([block_diag(w_a), block_diag(w_x)], axis=-1)
    row = lambda v: v[:, None, :]
    return dict(
        norm_g=row(norm_g), w_in=w_in2.astype(BF16), conv_w=conv_w, conv_b=row(conv_b),
        w_gate=w_gate.astype(BF16), b_a=row(b_a), b_x=row(b_x), lru_lambda=row(lru_lambda),
        q_norm_g=row(q_norm_g), w_uq=w_uq2.astype(BF16), kv_norm_g=row(kv_norm_g),
        w_ukt=jnp.transpose(w_uk, (0, 2, 3, 1)).astype(BF16),
        w_uv=jnp.transpose(w_uv, (0, 2, 1, 3)).astype(BF16),
        w_out=w_out.astype(BF16))


def kernel(x_prompt, x_sample, cache_kv_latent, cache_k_rope, page_table, state_rglru, state_conv,
           norm_g, w_in, conv_w, conv_b, w_a, b_a, w_x, b_x, lru_lambda, q_norm_g, w_uq,
           kv_norm_g, w_uk, w_uv, w_out, final_norm_g):
    bp, t, d_model = x_prompt.shape
    bs, ts, _ = x_sample.shape
    depth = w_in.shape[0]
    kv_rank = kv_norm_g.shape[-1]
    d_rg = conv_b.shape[-1]
    past_len = page_table.shape[1] * cache_kv_latent.shape[2]

    ws = _prepare_weights(norm_g, w_in, conv_w, conv_b, w_a, b_a, w_x, b_x, lru_lambda,
                          q_norm_g, w_uq, kv_norm_g, w_uk, w_uv, w_out)
    final_g = final_norm_g[None, :]
    cos_p, sin_p = _rope_tables(jnp.arange(t, dtype=jnp.int32))
    pos_s = past_len + jnp.arange(ts, dtype=jnp.int32)
    cos_s, sin_s = _rope_tables(jnp.tile(pos_s, bs))
    tab_blocks_p = t // min(ROW_TILE, bp * t)
    tab_blocks_s = (bs * ts) // min(ROW_TILE, bs * ts)
    buf_t = jnp.transpose(state_conv, (0, 2, 1, 3))
    cache_kr_t = jnp.swapaxes(cache_k_rope, 2, 3)

    x_p = x_prompt.reshape(bp * t, d_model)
    x_s = x_sample.reshape(bs * ts, d_model)
    kv_p, kr_p, h_p, cv_p = [], [], [], []
    kv_s, kr_s, h_s, cv_s = [], [], [], []
    for l in range(depth):
        final = l == depth - 1

        xr, sgr, sga, q, k, ckv, kpe = _proj(x_p, ws, l, cos_p, sin_p, tab_blocks_p)
        yr, hl, cv = _rglru_prompt(xr, sgr, ws, l, bp, t)
        o_lat = _attn_prompt(q, k, bp, t, kv_rank)
        x_p = _out(x_p, yr, o_lat, sga, ws, l, final_g, final)
        kv_p.append(ckv.reshape(bp, t, kv_rank)); kr_p.append(kpe.reshape(bp, t, QK_ROPE))
        h_p.append(hl); cv_p.append(cv)

        xr, sgr, sga, q, _, ckv, kpe = _proj(x_s, ws, l, cos_s, sin_s, tab_blocks_s)
        to_time_major = lambda v: jnp.transpose(v.reshape(bs, ts, -1), (1, 0, 2))
        yr_t, hl, cv_t = _rglru_sample(to_time_major(xr), to_time_major(sgr), buf_t, state_rglru, ws, l)
        yr = jnp.transpose(yr_t, (1, 0, 2)).reshape(bs * ts, d_rg)
        ckv = ckv.reshape(bs, ts, kv_rank)
        kpe = kpe.reshape(bs, ts, QK_ROPE)
        q_b = jnp.transpose(q.reshape(N_HEADS, bs, ts, -1), (1, 0, 2, 3)).reshape(bs, N_HEADS * ts, -1)
        o_b = _attn_sample(q_b, ckv, kpe, cache_kv_latent, cache_kr_t, page_table, l)
        o_lat = jnp.transpose(o_b.reshape(bs, N_HEADS, ts, kv_rank), (1, 0, 2, 3)).reshape(
            N_HEADS, bs * ts, kv_rank)
        x_s = _out(x_s, yr, o_lat, sga, ws, l, final_g, final)
        kv_s.append(ckv); kr_s.append(kpe); h_s.append(hl); cv_s.append(cv_t)

    y_prompt = x_p.reshape(bp, t, d_model)
    y_sample = x_s.reshape(bs, ts, d_model)
    conv_sample = jnp.transpose(jnp.stack(cv_s), (0, 2, 1, 3))
    return (y_prompt, y_sample,
            jnp.stack(kv_p), jnp.stack(kr_p), jnp.stack(h_p), jnp.stack(cv_p),
            jnp.stack(kv_s), jnp.stack(kr_s), jnp.stack(h_s), conv_sample)
```

```python
import functools

import jax
import jax.numpy as jnp
from jax import lax
from jax.experimental import pallas as pl
from jax.experimental.pallas import tpu as pltpu

F32 = jnp.float32
BF16 = jnp.bfloat16

N_RG_HEADS = 8
CONV_W = 4
LRU_C = 8.0
N_HEADS = 4
QK_NOPE = 128
QK_ROPE = 64
V_DIM = 128
ROPE_THETA = 10000.0
ATT_SCALE = (QK_NOPE + QK_ROPE) ** -0.5
EPS = 1e-6
MASK_VALUE = -1e30

VMEM_LIMIT_BYTES = 48 * 1024 * 1024

ROW_TILE = 512
LRU_CHUNK = 512
ATT_TQ = 512
SAMPLE_SECTIONS = 2
LANE = 128


def _params(*sem):
    return pltpu.CompilerParams(dimension_semantics=sem, vmem_limit_bytes=VMEM_LIMIT_BYTES)


def _layer_spec(w, layer):
    zeros = (0,) * (w.ndim - 1)
    return pl.BlockSpec((None,) + w.shape[1:], lambda *grid: (layer,) + zeros)


def _rms(v, g):
    return v * lax.rsqrt(jnp.mean(v * v, axis=-1, keepdims=True) + EPS) * g


def _silu(v):
    return v * jax.nn.sigmoid(v)


def _dot(a, b):
    return jnp.dot(a, b, preferred_element_type=F32)


def _dot_nt(a, b):
    return lax.dot_general(a, b, (((1,), (1,)), ((), ())), preferred_element_type=F32)


def _proj_kernel(x_ref, ng_ref, win_ref, qng_ref, wuq_ref, kvng_ref, wukt_ref, cos_ref, sin_ref,
                 xr_ref, sgr_ref, sga_ref, q_ref, k_ref, ckv_ref, kpe_ref, *, d_rg, q_rank, kv_rank):
    u = _rms(x_ref[...], ng_ref[...])
    z = _dot(u.astype(BF16), win_ref[...])
    o = 0
    xr_ref[...] = z[:, o:o + d_rg]; o += d_rg
    sgr_ref[...] = _silu(z[:, o:o + d_rg]); o += d_rg
    c_q = z[:, o:o + q_rank]; o += q_rank
    c_kv = z[:, o:o + kv_rank]; o += kv_rank
    d_att = N_HEADS * V_DIM
    sga_ref[...] = _silu(z[:, o:o + d_att]); o += d_att
    kpe_raw = z[:, o:o + QK_ROPE]; o += QK_ROPE
    kpe_swp = z[:, o:o + QK_ROPE]

    cos = cos_ref[...]
    sin = sin_ref[...]

    ckv_n = _rms(c_kv, kvng_ref[...])
    kpe = kpe_raw * cos[:, :QK_ROPE] + kpe_swp * sin[:, :QK_ROPE]
    ckv_ref[...] = ckv_n
    kpe_ref[...] = kpe
    rope_pad = jnp.zeros((x_ref.shape[0], LANE - QK_ROPE), BF16)
    k_ref[:, 0:kv_rank] = ckv_n.astype(BF16)
    k_ref[:, kv_rank:] = jnp.concatenate([kpe.astype(BF16), rope_pad], axis=-1)

    q = _dot(_rms(c_q, qng_ref[...]).astype(BF16), wuq_ref[...])
    pe0 = N_HEADS * QK_NOPE
    sw0 = pe0 + N_HEADS * QK_ROPE
    for pair in range(N_HEADS // 2):
        pe = q[:, pe0 + 128 * pair:pe0 + 128 * (pair + 1)]
        sw = q[:, sw0 + 128 * pair:sw0 + 128 * (pair + 1)]
        pe_r = (pe * cos + sw * sin) * ATT_SCALE
        for sub in range(2):
            h = 2 * pair + sub
            q_ref[h, :, kv_rank:] = jnp.concatenate(
                [pe_r[:, QK_ROPE * sub:QK_ROPE * (sub + 1)].astype(BF16), rope_pad], axis=-1)
    for h in range(N_HEADS):
        q_lat = _dot(q[:, QK_NOPE * h:QK_NOPE * (h + 1)].astype(BF16), wukt_ref[h])
        q_ref[h, :, 0:kv_rank] = (q_lat * ATT_SCALE).astype(BF16)


def _proj(x, ws, layer, cos_tab, sin_tab, tab_blocks):
    n, d_model = x.shape
    tm = min(ROW_TILE, n)
    d_rg = ws["conv_b"].shape[-1]
    q_rank = ws["q_norm_g"].shape[-1]
    kv_rank = ws["kv_norm_g"].shape[-1]
    d_att = N_HEADS * V_DIM
    d_qk = kv_rank + LANE
    rows = lambda w: pl.BlockSpec((tm, w), lambda i: (i, 0))
    tab = pl.BlockSpec((tm, LANE), lambda i: (i % tab_blocks, 0))
    names = ("norm_g", "w_in", "q_norm_g", "w_uq", "kv_norm_g", "w_ukt")
    kern = functools.partial(_proj_kernel, d_rg=d_rg, q_rank=q_rank, kv_rank=kv_rank)
    return pl.pallas_call(
        kern,
        grid=(n // tm,),
        in_specs=[rows(d_model)] + [_layer_spec(ws[k], layer) for k in names] + [tab, tab],
        out_specs=[rows(d_rg), rows(d_rg), rows(d_att),
                   pl.BlockSpec((N_HEADS, tm, d_qk), lambda i: (0, i, 0)),
                   rows(d_qk), rows(kv_rank), rows(QK_ROPE)],
        out_shape=[jax.ShapeDtypeStruct((n, d_rg), F32), jax.ShapeDtypeStruct((n, d_rg), F32),
                   jax.ShapeDtypeStruct((n, d_att), F32),
                   jax.ShapeDtypeStruct((N_HEADS, n, d_qk), BF16),
                   jax.ShapeDtypeStruct((n, d_qk), BF16),
                   jax.ShapeDtypeStruct((n, kv_rank), F32), jax.ShapeDtypeStruct((n, QK_ROPE), F32)],
        compiler_params=_params("arbitrary"),
        name="proj",
    )(x, *[ws[k] for k in names], cos_tab, sin_tab)


LRU_WEIGHTS = ("conv_w", "conv_b", "w_gate", "b_a", "b_x", "lru_lambda")


def _lru_gates(xc, wg_ref, ba_ref, bx_ref, lam_ref):
    d_rg = xc.shape[-1]
    g = _dot(xc.astype(BF16), wg_ref[...])
    r = jax.nn.sigmoid(g[:, :d_rg] + ba_ref[...])
    i = jax.nn.sigmoid(g[:, d_rg:] + bx_ref[...])
    neg_lam = -lam_ref[...]
    softplus = jnp.maximum(neg_lam, 0.0) + jnp.log1p(jnp.exp(-jnp.abs(neg_lam)))
    log_a = -LRU_C * r * softplus
    a = jnp.exp(log_a)
    return a, jnp.sqrt(-jnp.tanh(log_a) * (a * a + 1.0)) * i * xc


def _rglru_prompt_kernel(xr_ref, sgr_ref, cw_ref, cb_ref, wg_ref, ba_ref, bx_ref, lam_ref,
                         y_ref, hlast_ref, conv_ref, xpad_scr, a_scr, b_scr, h_scr, *, tc):
    ci = pl.program_id(1)
    d_rg = xr_ref.shape[-1]

    @pl.when(ci == 0)
    def _():
        xpad_scr[0:8, :] = jnp.zeros((8, d_rg), F32)
        h_scr[...] = jnp.zeros((8, d_rg), F32)

    xpad_scr[8:8 + tc, :] = xr_ref[...]
    xc = cb_ref[...] + cw_ref[0:1, :] * xpad_scr[5:5 + tc, :]
    for k in range(1, CONV_W):
        xc = xc + cw_ref[k:k + 1, :] * xpad_scr[5 + k:5 + k + tc, :]
    a, b = _lru_gates(xc, wg_ref, ba_ref, bx_ref, lam_ref)

    a = a.reshape(tc // 8, 8, d_rg)
    b = b.reshape(tc // 8, 8, d_rg)
    step = lax.broadcasted_iota(jnp.int32, a.shape, 1)
    for d in (1, 2, 4):
        keep = step >= d
        b = jnp.where(keep, a * pltpu.roll(b, d, 1) + b, b)
        a = jnp.where(keep, a * pltpu.roll(a, d, 1), a)
    a_scr[...] = a.reshape(tc, d_rg)
    b_scr[...] = b.reshape(tc, d_rg)

    def group(g, h_in):
        r0 = pl.multiple_of(g * 8, 8)
        hs = a_scr[pl.ds(r0, 8), :] * h_in + b_scr[pl.ds(r0, 8), :]
        b_scr[pl.ds(r0, 8), :] = hs
        return jnp.broadcast_to(hs[7:8, :], hs.shape)

    h_in = lax.fori_loop(0, tc // 8, group, h_scr[...], unroll=8)
    h_scr[...] = h_in
    y_ref[...] = (b_scr[...] * sgr_ref[...]).astype(y_ref.dtype)
    xpad_scr[0:8, :] = xpad_scr[tc:tc + 8, :]

    @pl.when(ci == pl.num_programs(1) - 1)
    def _():
        hlast_ref[...] = h_in[0:1, :]
        conv_ref[...] = xpad_scr[tc + 8 - (CONV_W - 1):tc + 8, :]


def _rglru_prompt(xr, sgr, ws, layer, bsz, t):
    d_rg = xr.shape[-1]
    tc = min(LRU_CHUNK, t)
    nc = t // tc
    rows = pl.BlockSpec((tc, d_rg), lambda b, c: (b * nc + c, 0))
    y, hlast, conv = pl.pallas_call(
        functools.partial(_rglru_prompt_kernel, tc=tc),
        grid=(bsz, nc),
        in_specs=[rows, rows] + [_layer_spec(ws[k], layer) for k in LRU_WEIGHTS],
        out_specs=[rows, pl.BlockSpec((None, 1, d_rg), lambda b, c: (b, 0, 0)),
                   pl.BlockSpec((None, CONV_W - 1, d_rg), lambda b, c: (b, 0, 0))],
        out_shape=[jax.ShapeDtypeStruct((bsz * t, d_rg), BF16),
                   jax.ShapeDtypeStruct((bsz, 1, d_rg), F32),
                   jax.ShapeDtypeStruct((bsz, CONV_W - 1, d_rg), F32)],
        scratch_shapes=[pltpu.VMEM((tc + 8, d_rg), F32), pltpu.VMEM((tc, d_rg), F32),
                        pltpu.VMEM((tc, d_rg), F32), pltpu.VMEM((8, d_rg), F32)],
        compiler_params=_params("arbitrary", "arbitrary"),
        name="rglru_prompt",
    )(xr, sgr, *[ws[k] for k in LRU_WEIGHTS])
    return y, hlast.reshape(bsz, d_rg), conv


def _rglru_sample_kernel(xr_ref, sgr_ref, buf_ref, h0_ref, cw_ref, cb_ref, wg_ref, ba_ref, bx_ref,
                         lam_ref, y_ref, hlast_ref, conv_ref):
    t_len = xr_ref.shape[0]
    xp = [buf_ref[k] for k in range(CONV_W - 1)] + [xr_ref[t] for t in range(t_len)]
    h = h0_ref[...]
    for t in range(t_len):
        xc = cb_ref[...] + cw_ref[0:1, :] * xp[t]
        for k in range(1, CONV_W):
            xc = xc + cw_ref[k:k + 1, :] * xp[t + k]
        a, b = _lru_gates(xc, wg_ref, ba_ref, bx_ref, lam_ref)
        h = a * h + b
        y_ref[t] = (h * sgr_ref[t]).astype(y_ref.dtype)
    hlast_ref[...] = h
    for k in range(CONV_W - 1):
        conv_ref[k] = xp[t_len + k]


def _rglru_sample(xr_t, sgr_t, buf_t, h0, ws, layer):
    t_len, bsz, d_rg = xr_t.shape
    whole = lambda shape: pl.BlockSpec(shape, lambda i: (0,) * len(shape))
    return pl.pallas_call(
        _rglru_sample_kernel,
        grid=(1,),
        in_specs=[whole(xr_t.shape), whole(sgr_t.shape), _layer_spec(buf_t, layer), _layer_spec(h0, layer)]
        + [_layer_spec(ws[k], layer) for k in LRU_WEIGHTS],
        out_specs=[whole((t_len, bsz, d_rg)), whole((bsz, d_rg)), whole((CONV_W - 1, bsz, d_rg))],
        out_shape=[jax.ShapeDtypeStruct((t_len, bsz, d_rg), BF16),
                   jax.ShapeDtypeStruct((bsz, d_rg), F32),
                   jax.ShapeDtypeStruct((CONV_W - 1, bsz, d_rg), F32)],
        compiler_params=_params("arbitrary"),
        name="rglru_sample",
    )(xr_t, sgr_t, buf_t, h0, *[ws[k] for k in LRU_WEIGHTS])


def _attn_prompt_kernel(q_ref, k_ref, o_ref, qt_scr, vt_scr, s_scr, m_scr, l_scr, acc_scr, *, tq, kv_rank):
    qi = pl.program_id(1)
    t = k_ref.shape[0]

    @pl.when(qi == 0)
    def _():
        for c in range(t // tq):
            blk = k_ref[c * tq:(c + 1) * tq, 0:kv_rank]
            vt_scr[:, c * tq:(c + 1) * tq] = blk.astype(F32).T.astype(BF16)

    for h in range(N_HEADS):
        qt_scr[:, h * tq:(h + 1) * tq] = q_ref[h].astype(F32).T.astype(BF16)
    m_scr[...] = jnp.full(m_scr.shape, MASK_VALUE, F32)
    l_scr[...] = jnp.zeros(l_scr.shape, F32)
    acc_scr[...] = jnp.zeros(acc_scr.shape, F32)

    def scores(ki):
        kb = k_ref[pl.ds(pl.multiple_of(ki * tq, tq), tq), :]
        return _dot(kb, qt_scr[...])

    def update(ki, s):
        m_prev = m_scr[...]
        m_new = jnp.maximum(m_prev, jnp.max(s, axis=0, keepdims=True))
        alpha = jnp.exp(m_prev - m_new)
        p = jnp.exp(s - m_new)
        l_scr[...] = alpha * l_scr[...] + jnp.sum(p, axis=0, keepdims=True)
        vt = vt_scr[:, pl.ds(pl.multiple_of(ki * tq, tq), tq)]
        acc_scr[...] = alpha * acc_scr[...] + _dot(vt, p.astype(BF16))
        m_scr[...] = m_new

    s_scr[...] = scores(0)

    def body(ki, c):
        s_next = scores(ki + 1)
        update(ki, s_scr[...])
        s_scr[...] = s_next
        return c

    lax.fori_loop(0, qi, body, 0)
    s = s_scr[...]
    k_off = lax.broadcasted_iota(jnp.int32, s.shape, 0)
    q_off = lax.broadcasted_iota(jnp.int32, s.shape, 1) & (tq - 1)
    update(qi, jnp.where(k_off <= q_off, s, MASK_VALUE))

    o_t = acc_scr[...] / l_scr[...]
    for h in range(N_HEADS):
        o_ref[h] = o_t[:, h * tq:(h + 1) * tq].T.astype(o_ref.dtype)


def _attn_prompt(q, k, bsz, t, kv_rank):
    d_qk = q.shape[-1]
    tq = min(ATT_TQ, t)
    assert tq & (tq - 1) == 0 and t % tq == 0
    nq = t // tq
    cols = N_HEADS * tq
    return pl.pallas_call(
        functools.partial(_attn_prompt_kernel, tq=tq, kv_rank=kv_rank),
        grid=(bsz, nq),
        in_specs=[pl.BlockSpec((N_HEADS, tq, d_qk), lambda b, i: (0, b * nq + i, 0)),
                  pl.BlockSpec((t, d_qk), lambda b, i: (b, 0))],
        out_specs=pl.BlockSpec((N_HEADS, tq, kv_rank), lambda b, i: (0, b * nq + i, 0)),
        out_shape=jax.ShapeDtypeStruct((N_HEADS, bsz * t, kv_rank), BF16),
        scratch_shapes=[pltpu.VMEM((d_qk, cols), BF16), pltpu.VMEM((kv_rank, t), BF16),
                        pltpu.VMEM((tq, cols), F32), pltpu.VMEM((1, cols), F32),
                        pltpu.VMEM((1, cols), F32), pltpu.VMEM((kv_rank, cols), F32)],
        compiler_params=_params("arbitrary", "arbitrary"),
        name="attn_prompt",
    )(q, k)


def _attn_sample_kernel(pt_ref, q_ref, ccur_ref, pcur_ref, kv_hbm, kr_hbm, o_ref,
                        kv_buf, kr_buf, cc_scr, pc_scr, sem,
                        *, layer, n_pages, page, t_len, kv_rank):
    b = pl.program_id(0)
    slot = lax.rem(b, 2)

    def page_copies(seq, sl, j):
        pg = pt_ref[seq, j]
        off = j * page
        return (pltpu.make_async_copy(kv_hbm.at[layer, pg], kv_buf.at[sl, pl.ds(off, page), :], sem.at[sl, 0]),
                pltpu.make_async_copy(kr_hbm.at[layer, pg], kr_buf.at[sl, :, pl.ds(off, page)], sem.at[sl, 1]))

    def fetch(seq, sl):
        for j in range(n_pages):
            for cp in page_copies(seq, sl, j):
                cp.start(priority=j % 2)

    def wait_all(sl):
        pltpu.make_async_copy(kv_buf.at[sl], kv_buf.at[sl], sem.at[sl, 0]).wait()
        pltpu.make_async_copy(kr_buf.at[sl], kr_buf.at[sl], sem.at[sl, 1]).wait()

    @pl.when(b == 0)
    def _():
        fetch(0, 0)

    wait_all(slot)

    @pl.when(b + 1 < pl.num_programs(0))
    def _():
        fetch(b + 1, 1 - slot)

    q = q_ref[...]
    q_lat = q[:, :kv_rank]
    q_pe = q[:, kv_rank:kv_rank + QK_ROPE]

    def local(s, v):
        m = jnp.max(s, axis=-1, keepdims=True)
        p = jnp.exp(s - m)
        return m, jnp.sum(p, axis=-1, keepdims=True), _dot(p.astype(BF16), v)

    cc_scr[...] = jnp.zeros(cc_scr.shape, F32)
    pc_scr[...] = jnp.zeros(pc_scr.shape, F32)
    cc_scr[0:t_len, :] = ccur_ref[...]
    pc_scr[0:t_len, :] = pcur_ref[...]
    vc = cc_scr[...].astype(BF16)
    s_cur = _dot_nt(q_lat, vc) + _dot_nt(q_pe, pc_scr[...].astype(BF16))
    q_step = lax.rem(lax.broadcasted_iota(jnp.int32, s_cur.shape, 0), t_len)
    k_step = lax.broadcasted_iota(jnp.int32, s_cur.shape, 1)
    s_cur = jnp.where(k_step <= q_step, s_cur, MASK_VALUE)

    parts = [local(s_cur, vc)]
    sec = (n_pages // SAMPLE_SECTIONS) * page
    for i in range(SAMPLE_SECTIONS):
        r = slice(i * sec, (i + 1) * sec)
        kc = kv_buf[slot, r, :]
        s = _dot(q_lat, kc.T.astype(BF16)) + _dot(q_pe, kr_buf[slot, :, r].astype(BF16))
        parts.append(local(s, kc.astype(BF16)))
    m = functools.reduce(jnp.maximum, [pm for pm, _, _ in parts])
    w = [jnp.exp(pm - m) for pm, _, _ in parts]
    l = sum(wi * pl_ for wi, (_, pl_, _) in zip(w, parts))
    o = sum(wi * po for wi, (_, _, po) in zip(w, parts))
    o_ref[...] = (o / l).astype(o_ref.dtype)


def _attn_sample(q, ckv, kpe, cache_kv, cache_kr_t, page_table, layer):
    bsz, rows, d_qk = q.shape
    t_len = ckv.shape[1]
    kv_rank = ckv.shape[-1]
    page = cache_kv.shape[2]
    n_pages = page_table.shape[1]
    assert n_pages % SAMPLE_SECTIONS == 0
    past = n_pages * page
    kern = functools.partial(_attn_sample_kernel, layer=layer, n_pages=n_pages, page=page,
                             t_len=t_len, kv_rank=kv_rank)
    return pl.pallas_call(
        kern,
        grid_spec=pltpu.PrefetchScalarGridSpec(
            num_scalar_prefetch=1,
            grid=(bsz,),
            in_specs=[pl.BlockSpec((None, rows, d_qk), lambda b, pt: (b, 0, 0)),
                      pl.BlockSpec((None, t_len, kv_rank), lambda b, pt: (b, 0, 0)),
                      pl.BlockSpec((None, t_len, QK_ROPE), lambda b, pt: (b, 0, 0)),
                      pl.BlockSpec(memory_space=pl.ANY), pl.BlockSpec(memory_space=pl.ANY)],
            out_specs=pl.BlockSpec((None, rows, kv_rank), lambda b, pt: (b, 0, 0)),
            scratch_shapes=[pltpu.VMEM((2, past, kv_rank), F32), pltpu.VMEM((2, QK_ROPE, past), F32),
                            pltpu.VMEM((page, kv_rank), F32), pltpu.VMEM((page, QK_ROPE), F32),
                            pltpu.SemaphoreType.DMA((2, 2))]),
        out_shape=jax.ShapeDtypeStruct((bsz, rows, kv_rank), BF16),
        compiler_params=_params("arbitrary"),
        name="attn_sample",
    )(page_table, q, ckv, kpe, cache_kv, cache_kr_t)


def _out_kernel(x_ref, yr_ref, o_ref, sga_ref, wuv_ref, wout_ref, fg_ref, out_ref, *, final):
    d_rg = yr_ref.shape[-1]
    sga = sga_ref[...]
    acc = x_ref[...] + _dot(yr_ref[...], wout_ref[0:d_rg, :])
    y_a = [(_dot(o_ref[h], wuv_ref[h]) * sga[:, V_DIM * h:V_DIM * (h + 1)]).astype(BF16)
           for h in range(N_HEADS)]
    acc = acc + _dot(jnp.concatenate(y_a, axis=-1), wout_ref[d_rg:, :])
    out_ref[...] = _rms(acc, fg_ref[...]) if final else acc


def _out(x, yr, o_lat, sga, ws, layer, final_g, final):
    n, d_model = x.shape
    tm = min(ROW_TILE, n)
    d_rg = yr.shape[-1]
    kv_rank = o_lat.shape[-1]
    d_att = sga.shape[-1]
    rows = lambda w: pl.BlockSpec((tm, w), lambda i: (i, 0))
    return pl.pallas_call(
        functools.partial(_out_kernel, final=final),
        grid=(n // tm,),
        in_specs=[rows(d_model), rows(d_rg), pl.BlockSpec((N_HEADS, tm, kv_rank), lambda i: (0, i, 0)),
                  rows(d_att), _layer_spec(ws["w_uv"], layer), _layer_spec(ws["w_out"], layer),
                  pl.BlockSpec((1, d_model), lambda i: (0, 0))],
        out_specs=rows(d_model),
        out_shape=jax.ShapeDtypeStruct((n, d_model), F32),
        compiler_params=_params("arbitrary"),
        name="out",
    )(x, yr, o_lat, sga, ws["w_uv"], ws["w_out"], final_g)


def _rope_tables(pos):
    half = QK_ROPE // 2
    freqs = ROPE_THETA ** (-jnp.arange(half, dtype=F32) / half)
    ang = pos.astype(F32)[:, None] * freqs
    cos, sin = jnp.cos(ang), jnp.sin(ang)
    return jnp.concatenate([cos, cos, cos, cos], axis=-1), jnp.concatenate([-sin, sin, -sin, sin], axis=-1)


def _swap_halves(w):
    half = w.shape[-1] // 2
    return jnp.concatenate([w[..., half:], w[..., :half]], axis=-1)


def _prepare_weights(norm_g, w_in, conv_w, conv_b, w_a, b_a, w_x, b_x, lru_lambda, q_norm_g, w_uq,
                     kv_norm_g, w_uk, w_uv, w_out):
    depth, d_model, _ = w_in.shape
    d_rg = conv_b.shape[-1]
    q_rank = q_norm_g.shape[-1]
    kv_rank = kv_norm_g.shape[-1]
    s3 = 2 * d_rg + q_rank + kv_rank
    s4 = s3 + QK_ROPE
    k_pe_cols = w_in[..., s3:s4]
    w_in2 = jnp.concatenate([w_in[..., :s3], w_in[..., s4:], k_pe_cols, _swap_halves(k_pe_cols)], axis=-1)
    wq = w_uq.reshape(depth, q_rank, N_HEADS, QK_NOPE + QK_ROPE)
    q_pe_cols = wq[..., QK_NOPE:]
    w_uq2 = jnp.concatenate([wq[..., :QK_NOPE].reshape(depth, q_rank, -1),
                             q_pe_cols.reshape(depth, q_rank, -1),
                             _swap_halves(q_pe_cols).reshape(depth, q_rank, -1)], axis=-1)
    eye = jnp.eye(N_RG_HEADS, dtype=w_a.dtype)
    block_diag = lambda w: jnp.einsum("lhij,hg->lhigj", w, eye).reshape(depth, d_rg, d_rg)
    w_gate = jnp.concatenate([block_diag(w_a), block_diag(w_x)], axis=-1)
    row = lambda v: v[:, None, :]
    return dict(
        norm_g=row(norm_g), w_in=w_in2.astype(BF16), conv_w=conv_w, conv_b=row(conv_b),
        w_gate=w_gate.astype(BF16), b_a=row(b_a), b_x=row(b_x), lru_lambda=row(lru_lambda),
        q_norm_g=row(q_norm_g), w_uq=w_uq2.astype(BF16), kv_norm_g=row(kv_norm_g),
        w_ukt=jnp.transpose(w_uk, (0, 2, 3, 1)).astype(BF16),
        w_uv=jnp.transpose(w_uv, (0, 2, 1, 3)).astype(BF16),
        w_out=w_out.astype(BF16))


def kernel(x_prompt, x_sample, cache_kv_latent, cache_k_rope, page_table, state_rglru, state_conv,
           norm_g, w_in, conv_w, conv_b, w_a, b_a, w_x, b_x, lru_lambda, q_norm_g, w_uq,
           kv_norm_g, w_uk, w_uv, w_out, final_norm_g):
    bp, t, d_model = x_prompt.shape
    bs, ts, _ = x_sample.shape
    depth = w_in.shape[0]
    kv_rank = kv_norm_g.shape[-1]
    d_rg = conv_b.shape[-1]
    past_len = page_table.shape[1] * cache_kv_latent.shape[2]

    ws = _prepare_weights(norm_g, w_in, conv_w, conv_b, w_a, b_a, w_x, b_x, lru_lambda,
                          q_norm_g, w_uq, kv_norm_g, w_uk, w_uv, w_out)
    final_g = final_norm_g[None, :]
    cos_p, sin_p = _rope_tables(jnp.arange(t, dtype=jnp.int32))
    pos_s = past_len + jnp.arange(ts, dtype=jnp.int32)
    cos_s, sin_s = _rope_tables(jnp.tile(pos_s, bs))
    tab_blocks_p = t // min(ROW_TILE, bp * t)
    tab_blocks_s = (bs * ts) // min(ROW_TILE, bs * ts)
    buf_t = jnp.transpose(state_conv, (0, 2, 1, 3))
    cache_kr_t = jnp.swapaxes(cache_k_rope, 2, 3)

    x_p = x_prompt.reshape(bp * t, d_model)
    x_s = x_sample.reshape(bs * ts, d_model)
    kv_p, kr_p, h_p, cv_p = [], [], [], []
    kv_s, kr_s, h_s, cv_s = [], [], [], []
    for l in range(depth):
        final = l == depth - 1

        xr, sgr, sga, q, k, ckv, kpe = _proj(x_p, ws, l, cos_p, sin_p, tab_blocks_p)
        yr, hl, cv = _rglru_prompt(xr, sgr, ws, l, bp, t)
        o_lat = _attn_prompt(q, k, bp, t, kv_rank)
        x_p = _out(x_p, yr, o_lat, sga, ws, l, final_g, final)
        kv_p.append(ckv.reshape(bp, t, kv_rank)); kr_p.append(kpe.reshape(bp, t, QK_ROPE))
        h_p.append(hl); cv_p.append(cv)

        xr, sgr, sga, q, _, ckv, kpe = _proj(x_s, ws, l, cos_s, sin_s, tab_blocks_s)
        to_time_major = lambda v: jnp.transpose(v.reshape(bs, ts, -1), (1, 0, 2))
        yr_t, hl, cv_t = _rglru_sample(to_time_major(xr), to_time_major(sgr), buf_t, state_rglru, ws, l)
        yr = jnp.transpose(yr_t, (1, 0, 2)).reshape(bs * ts, d_rg)
        ckv = ckv.reshape(bs, ts, kv_rank)
        kpe = kpe.reshape(bs, ts, QK_ROPE)
        q_b = jnp.transpose(q.reshape(N_HEADS, bs, ts, -1), (1, 0, 2, 3)).reshape(bs, N_HEADS * ts, -1)
        o_b = _attn_sample(q_b, ckv, kpe, cache_kv_latent, cache_kr_t, page_table, l)
        o_lat = jnp.transpose(o_b.reshape(bs, N_HEADS, ts, kv_rank), (1, 0, 2, 3)).reshape(
            N_HEADS, bs * ts, kv_rank)
        x_s = _out(x_s, yr, o_lat, sga, ws, l, final_g, final)
        kv_s.append(ckv); kr_s.append(kpe); h_s.append(hl); cv_s.append(cv_t)

    y_prompt = x_p.reshape(bp, t, d_model)
    y_sample = x_s.reshape(bs, ts, d_model)
    conv_sample = jnp.transpose(jnp.stack(cv_s), (0, 2, 1, 3))
    return (y_prompt, y_sample,
            jnp.stack(kv_p), jnp.stack(kr_p), jnp.stack(h_p), jnp.stack(cv_p),
            jnp.stack(kv_s), jnp.stack(kr_s), jnp.stack(h_s), conv_sample)
```

```python
import functools

import jax
import jax.numpy as jnp
from jax import lax
from jax.experimental import pallas as pl
from jax.experimental.pallas import tpu as pltpu

F32 = jnp.float32
BF16 = jnp.bfloat16

N_RG_HEADS = 8
CONV_W = 4
LRU_C = 8.0
N_HEADS = 4
QK_NOPE = 128
QK_ROPE = 64
V_DIM = 128
ROPE_THETA = 10000.0
ATT_SCALE = (QK_NOPE + QK_ROPE) ** -0.5
EPS = 1e-6
MASK_VALUE = -1e30

VMEM_LIMIT_BYTES = 48 * 1024 * 1024

ROW_TILE = 512
ATT_TQ = 512
SAMPLE_SECTIONS = 2
LANE = 128


def _params(*sem):
    return pltpu.CompilerParams(dimension_semantics=sem, vmem_limit_bytes=VMEM_LIMIT_BYTES)


def _layer_spec(w, layer):
    zeros = (0,) * (w.ndim - 1)
    return pl.BlockSpec((None,) + w.shape[1:], lambda *grid: (layer,) + zeros)


def _rms(v, g):
    return v * lax.rsqrt(jnp.mean(v * v, axis=-1, keepdims=True) + EPS) * g


def _silu(v):
    return v * jax.nn.sigmoid(v)


def _dot(a, b):
    return jnp.dot(a, b, preferred_element_type=F32)


def _dot_nt(a, b):
    return lax.dot_general(a, b, (((1,), (1,)), ((), ())), preferred_element_type=F32)


def _proj_kernel(x_ref, ng_ref, win_ref, qng_ref, wuq_ref, kvng_ref, wukt_ref, cos_ref, sin_ref,
                 xr_ref, sgr_ref, sga_ref, q_ref, k_ref, ckv_ref, kpe_ref, *, d_rg, q_rank, kv_rank,
                 x=None):
    u = _rms(x_ref[...] if x is None else x, ng_ref[...])
    z = _dot(u.astype(BF16), win_ref[...])
    o = 0
    xr_ref[...] = z[:, o:o + d_rg]; o += d_rg
    sgr_ref[...] = _silu(z[:, o:o + d_rg]); o += d_rg
    c_q = z[:, o:o + q_rank]; o += q_rank
    c_kv = z[:, o:o + kv_rank]; o += kv_rank
    d_att = N_HEADS * V_DIM
    sga_ref[...] = _silu(z[:, o:o + d_att]); o += d_att
    kpe_raw = z[:, o:o + QK_ROPE]; o += QK_ROPE
    kpe_swp = z[:, o:o + QK_ROPE]

    cos = cos_ref[...]
    sin = sin_ref[...]

    ckv_n = _rms(c_kv, kvng_ref[...])
    kpe = kpe_raw * cos[:, :QK_ROPE] + kpe_swp * sin[:, :QK_ROPE]
    ckv_ref[...] = ckv_n
    kpe_ref[...] = kpe
    rope_pad = jnp.zeros((u.shape[0], LANE - QK_ROPE), BF16)
    k_ref[:, 0:kv_rank] = ckv_n.astype(BF16)
    k_ref[:, kv_rank:] = jnp.concatenate([kpe.astype(BF16), rope_pad], axis=-1)

    q = _dot(_rms(c_q, qng_ref[...]).astype(BF16), wuq_ref[...])
    pe0 = N_HEADS * QK_NOPE
    sw0 = pe0 + N_HEADS * QK_ROPE
    for pair in range(N_HEADS // 2):
        pe = q[:, pe0 + 128 * pair:pe0 + 128 * (pair + 1)]
        sw = q[:, sw0 + 128 * pair:sw0 + 128 * (pair + 1)]
        pe_r = (pe * cos + sw * sin) * ATT_SCALE
        for sub in range(2):
            h = 2 * pair + sub
            q_ref[h, :, kv_rank:] = jnp.concatenate(
                [pe_r[:, QK_ROPE * sub:QK_ROPE * (sub + 1)].astype(BF16), rope_pad], axis=-1)
    for h in range(N_HEADS):
        q_lat = _dot(q[:, QK_NOPE * h:QK_NOPE * (h + 1)].astype(BF16), wukt_ref[h])
        q_ref[h, :, 0:kv_rank] = (q_lat * ATT_SCALE).astype(BF16)


PROJ_WEIGHTS = ("norm_g", "w_in", "q_norm_g", "w_uq", "kv_norm_g", "w_ukt")


def _proj_pieces(ws, layer, n, tm, tab_blocks):
    d_rg = ws["conv_b"].shape[-1]
    q_rank = ws["q_norm_g"].shape[-1]
    kv_rank = ws["kv_norm_g"].shape[-1]
    d_att = N_HEADS * V_DIM
    d_qk = kv_rank + LANE
    rows = lambda w: pl.BlockSpec((tm, w), lambda i: (i, 0))
    tab = pl.BlockSpec((tm, LANE), lambda i: (i % tab_blocks, 0))
    in_specs = [_layer_spec(ws[k], layer) for k in PROJ_WEIGHTS] + [tab, tab]
    out_specs = [rows(d_rg), rows(d_rg), rows(d_att),
                 pl.BlockSpec((N_HEADS, tm, d_qk), lambda i: (0, i, 0)),
                 rows(d_qk), rows(kv_rank), rows(QK_ROPE)]
    out_shape = [jax.ShapeDtypeStruct((n, d_rg), F32), jax.ShapeDtypeStruct((n, d_rg), F32),
                 jax.ShapeDtypeStruct((n, d_att), F32),
                 jax.ShapeDtypeStruct((N_HEADS, n, d_qk), BF16),
                 jax.ShapeDtypeStruct((n, d_qk), BF16),
                 jax.ShapeDtypeStruct((n, kv_rank), F32), jax.ShapeDtypeStruct((n, QK_ROPE), F32)]
    return in_specs, out_specs, out_shape, dict(d_rg=d_rg, q_rank=q_rank, kv_rank=kv_rank)


def _proj(x, ws, layer, cos_tab, sin_tab, tab_blocks):
    n, d_model = x.shape
    tm = min(ROW_TILE, n)
    in_specs, out_specs, out_shape, dims = _proj_pieces(ws, layer, n, tm, tab_blocks)
    return pl.pallas_call(
        functools.partial(_proj_kernel, **dims),
        grid=(n // tm,),
        in_specs=[pl.BlockSpec((tm, d_model), lambda i: (i, 0))] + in_specs,
        out_specs=out_specs,
        out_shape=out_shape,
        compiler_params=_params("arbitrary"),
        name="proj",
    )(x, *[ws[k] for k in PROJ_WEIGHTS], cos_tab, sin_tab)


LRU_WEIGHTS = ("conv_w", "conv_b", "w_gate", "b_a", "b_x", "lru_lambda")


def _lru_gates(xc, wg_ref, ba_ref, bx_ref, lam_ref):
    d_rg = xc.shape[-1]
    g = _dot(xc.astype(BF16), wg_ref[...])
    r = jax.nn.sigmoid(g[:, :d_rg] + ba_ref[...])
    i = jax.nn.sigmoid(g[:, d_rg:] + bx_ref[...])
    neg_lam = -lam_ref[...]
    softplus = jnp.maximum(neg_lam, 0.0) + jnp.log1p(jnp.exp(-jnp.abs(neg_lam)))
    log_a = -LRU_C * r * softplus
    a = jnp.exp(log_a)
    return a, jnp.sqrt(-jnp.tanh(log_a) * (a * a + 1.0)) * i * xc


def _lru_chunk(ci, nc, xr_ref, sgr_ref, cw_ref, cb_ref, wg_ref, ba_ref, bx_ref, lam_ref,
               y_ref, hlast_ref, conv_ref, xpad_scr, a_scr, b_scr, h_scr):
    tc, d_rg = xr_ref.shape

    @pl.when(ci == 0)
    def _():
        xpad_scr[0:8, :] = jnp.zeros((8, d_rg), F32)
        h_scr[...] = jnp.zeros((8, d_rg), F32)

    xpad_scr[8:8 + tc, :] = xr_ref[...]
    xc = cb_ref[...] + cw_ref[0:1, :] * xpad_scr[5:5 + tc, :]
    for k in range(1, CONV_W):
        xc = xc + cw_ref[k:k + 1, :] * xpad_scr[5 + k:5 + k + tc, :]
    a, b = _lru_gates(xc, wg_ref, ba_ref, bx_ref, lam_ref)

    a = a.reshape(tc // 8, 8, d_rg)
    b = b.reshape(tc // 8, 8, d_rg)
    step = lax.broadcasted_iota(jnp.int32, a.shape, 1)
    for d in (1, 2, 4):
        keep = step >= d
        b = jnp.where(keep, a * pltpu.roll(b, d, 1) + b, b)
        a = jnp.where(keep, a * pltpu.roll(a, d, 1), a)
    a_scr[...] = a.reshape(tc, d_rg)
    b_scr[...] = b.reshape(tc, d_rg)

    def group(g, h_in):
        r0 = pl.multiple_of(g * 8, 8)
        hs = a_scr[pl.ds(r0, 8), :] * h_in + b_scr[pl.ds(r0, 8), :]
        b_scr[pl.ds(r0, 8), :] = hs
        return jnp.broadcast_to(hs[7:8, :], hs.shape)

    h_in = lax.fori_loop(0, tc // 8, group, h_scr[...], unroll=8)
    h_scr[...] = h_in
    y_ref[...] = (b_scr[...] * sgr_ref[...]).astype(y_ref.dtype)
    xpad_scr[0:8, :] = xpad_scr[tc:tc + 8, :]

    @pl.when(ci == nc - 1)
    def _():
        hlast_ref[...] = h_in[0:1, :]
        conv_ref[...] = xpad_scr[tc + 8 - (CONV_W - 1):tc + 8, :]


def _lru_prompt_pieces(xr, ws, layer, bsz, t, tc, step_index):
    d_rg = xr.shape[-1]
    nc = t // tc
    rows = pl.BlockSpec((tc, d_rg), lambda *g: (step_index(*g), 0))
    seq = lambda w: pl.BlockSpec((None, w, d_rg), lambda *g: (step_index(*g) // nc, 0, 0))
    in_specs = [rows, rows] + [_layer_spec(ws[k], layer) for k in LRU_WEIGHTS]
    out_specs = [rows, seq(1), seq(CONV_W - 1)]
    out_shape = [jax.ShapeDtypeStruct((bsz * t, d_rg), BF16),
                 jax.ShapeDtypeStruct((bsz, 1, d_rg), F32),
                 jax.ShapeDtypeStruct((bsz, CONV_W - 1, d_rg), F32)]
    scratch = [pltpu.VMEM((tc + 8, d_rg), F32), pltpu.VMEM((tc, d_rg), F32),
               pltpu.VMEM((tc, d_rg), F32), pltpu.VMEM((8, d_rg), F32)]
    return in_specs, out_specs, out_shape, scratch


def _rglru_sample_kernel(xr_ref, sgr_ref, buf_ref, h0_ref, cw_ref, cb_ref, wg_ref, ba_ref, bx_ref,
                         lam_ref, y_ref, hlast_ref, conv_ref):
    t_len = xr_ref.shape[0]
    xp = [buf_ref[k] for k in range(CONV_W - 1)] + [xr_ref[t] for t in range(t_len)]
    h = h0_ref[...]
    for t in range(t_len):
        xc = cb_ref[...] + cw_ref[0:1, :] * xp[t]
        for k in range(1, CONV_W):
            xc = xc + cw_ref[k:k + 1, :] * xp[t + k]
        a, b = _lru_gates(xc, wg_ref, ba_ref, bx_ref, lam_ref)
        h = a * h + b
        y_ref[t] = (h * sgr_ref[t]).astype(y_ref.dtype)
    hlast_ref[...] = h
    for k in range(CONV_W - 1):
        conv_ref[k] = xp[t_len + k]


def _rglru_sample(xr_t, sgr_t, buf_t, h0, ws, layer):
    t_len, bsz, d_rg = xr_t.shape
    whole = lambda shape: pl.BlockSpec(shape, lambda i: (0,) * len(shape))
    return pl.pallas_call(
        _rglru_sample_kernel,
        grid=(1,),
        in_specs=[whole(xr_t.shape), whole(sgr_t.shape), _layer_spec(buf_t, layer), _layer_spec(h0, layer)]
        + [_layer_spec(ws[k], layer) for k in LRU_WEIGHTS],
        out_specs=[whole((t_len, bsz, d_rg)), whole((bsz, d_rg)), whole((CONV_W - 1, bsz, d_rg))],
        out_shape=[jax.ShapeDtypeStruct((t_len, bsz, d_rg), BF16),
                   jax.ShapeDtypeStruct((bsz, d_rg), F32),
                   jax.ShapeDtypeStruct((CONV_W - 1, bsz, d_rg), F32)],
        compiler_params=_params("arbitrary"),
        name="rglru_sample",
    )(xr_t, sgr_t, buf_t, h0, *[ws[k] for k in LRU_WEIGHTS])


def _attn_prompt_kernel(q_ref, k_ref, o_ref, qt_scr, vt_scr, s_scr, m_scr, l_scr, acc_scr, *, tq, kv_rank):
    qi = pl.program_id(1)
    t = k_ref.shape[0]

    @pl.when(qi == 0)
    def _():
        for c in range(t // tq):
            blk = k_ref[c * tq:(c + 1) * tq, 0:kv_rank]
            vt_scr[:, c * tq:(c + 1) * tq] = blk.astype(F32).T.astype(BF16)

    for h in range(N_HEADS):
        qt_scr[:, h * tq:(h + 1) * tq] = q_ref[h].astype(F32).T.astype(BF16)
    m_scr[...] = jnp.full(m_scr.shape, MASK_VALUE, F32)
    l_scr[...] = jnp.zeros(l_scr.shape, F32)
    acc_scr[...] = jnp.zeros(acc_scr.shape, F32)

    def scores(ki):
        kb = k_ref[pl.ds(pl.multiple_of(ki * tq, tq), tq), :]
        return _dot(kb, qt_scr[...])

    def update(ki, s):
        m_prev = m_scr[...]
        m_new = jnp.maximum(m_prev, jnp.max(s, axis=0, keepdims=True))
        alpha = jnp.exp(m_prev - m_new)
        p = jnp.exp(s - m_new)
        l_scr[...] = alpha * l_scr[...] + jnp.sum(p, axis=0, keepdims=True)
        vt = vt_scr[:, pl.ds(pl.multiple_of(ki * tq, tq), tq)]
        acc_scr[...] = alpha * acc_scr[...] + _dot(vt, p.astype(BF16))
        m_scr[...] = m_new

    s_scr[...] = scores(0)

    def body(ki, c):
        s_next = scores(ki + 1)
        update(ki, s_scr[...])
        s_scr[...] = s_next
        return c

    lax.fori_loop(0, qi, body, 0)
    s = s_scr[...]
    k_off = lax.broadcasted_iota(jnp.int32, s.shape, 0)
    q_off = lax.broadcasted_iota(jnp.int32, s.shape, 1) & (tq - 1)
    update(qi, jnp.where(k_off <= q_off, s, MASK_VALUE))

    o_t = acc_scr[...] / l_scr[...]
    for h in range(N_HEADS):
        o_ref[h] = o_t[:, h * tq:(h + 1) * tq].T.astype(o_ref.dtype)


def _attn_prompt(q, k, bsz, t, kv_rank):
    d_qk = q.shape[-1]
    tq = min(ATT_TQ, t)
    assert tq & (tq - 1) == 0 and t % tq == 0
    nq = t // tq
    cols = N_HEADS * tq
    return pl.pallas_call(
        functools.partial(_attn_prompt_kernel, tq=tq, kv_rank=kv_rank),
        grid=(bsz, nq),
        in_specs=[pl.BlockSpec((N_HEADS, tq, d_qk), lambda b, i: (0, b * nq + i, 0)),
                  pl.BlockSpec((t, d_qk), lambda b, i: (b, 0))],
        out_specs=pl.BlockSpec((N_HEADS, tq, kv_rank), lambda b, i: (0, b * nq + i, 0)),
        out_shape=jax.ShapeDtypeStruct((N_HEADS, bsz * t, kv_rank), BF16),
        scratch_shapes=[pltpu.VMEM((d_qk, cols), BF16), pltpu.VMEM((kv_rank, t), BF16),
                        pltpu.VMEM((tq, cols), F32), pltpu.VMEM((1, cols), F32),
                        pltpu.VMEM((1, cols), F32), pltpu.VMEM((kv_rank, cols), F32)],
        compiler_params=_params("arbitrary", "arbitrary"),
        name="attn_prompt",
    )(q, k)


def _attn_sample_kernel(pt_ref, *refs, layer, n_pages, page, t_len, kv_rank, lru_chunks):
    q_ref, ccur_ref, pcur_ref, kv_hbm, kr_hbm = refs[:5]
    lru_in, o_ref, lru_out = refs[5:13], refs[13], refs[14:17]
    kv_buf, kr_buf, cc_scr, pc_scr, sem = refs[17:22]
    lru_scr = refs[22:]
    b = pl.program_id(0)
    slot = lax.rem(b, 2)

    def page_copies(seq, sl, j):
        pg = pt_ref[seq, j]
        off = j * page
        return (pltpu.make_async_copy(kv_hbm.at[layer, pg], kv_buf.at[sl, pl.ds(off, page), :], sem.at[sl, 0]),
                pltpu.make_async_copy(kr_hbm.at[layer, pg], kr_buf.at[sl, :, pl.ds(off, page)], sem.at[sl, 1]))

    def fetch(seq, sl):
        for j in range(n_pages):
            for cp in page_copies(seq, sl, j):
                cp.start()

    def wait_all(sl):
        pltpu.make_async_copy(kv_buf.at[sl], kv_buf.at[sl], sem.at[sl, 0]).wait()
        pltpu.make_async_copy(kr_buf.at[sl], kr_buf.at[sl], sem.at[sl, 1]).wait()

    @pl.when(b == 0)
    def _():
        fetch(0, 0)

    wait_all(slot)

    @pl.when(b + 1 < pl.num_programs(0))
    def _():
        fetch(b + 1, 1 - slot)

    q = q_ref[...]
    q_lat = q[:, :kv_rank]
    q_pe = q[:, kv_rank:kv_rank + QK_ROPE]

    def local(s, v):
        m = jnp.max(s, axis=-1, keepdims=True)
        p = jnp.exp(s - m)
        return m, jnp.sum(p, axis=-1, keepdims=True), _dot(p.astype(BF16), v)

    cc_scr[...] = jnp.zeros(cc_scr.shape, F32)
    pc_scr[...] = jnp.zeros(pc_scr.shape, F32)
    cc_scr[0:t_len, :] = ccur_ref[...]
    pc_scr[0:t_len, :] = pcur_ref[...]
    vc = cc_scr[...].astype(BF16)
    s_cur = _dot_nt(q_lat, vc) + _dot_nt(q_pe, pc_scr[...].astype(BF16))
    q_step = lax.rem(lax.broadcasted_iota(jnp.int32, s_cur.shape, 0), t_len)
    k_step = lax.broadcasted_iota(jnp.int32, s_cur.shape, 1)
    s_cur = jnp.where(k_step <= q_step, s_cur, MASK_VALUE)

    parts = [local(s_cur, vc)]
    sec = (n_pages // SAMPLE_SECTIONS) * page
    for i in range(SAMPLE_SECTIONS):
        r = slice(i * sec, (i + 1) * sec)
        kc = kv_buf[slot, r, :]
        s = _dot(q_lat, kc.T.astype(BF16)) + _dot(q_pe, kr_buf[slot, :, r].astype(BF16))
        parts.append(local(s, kc.astype(BF16)))
    m = functools.reduce(jnp.maximum, [pm for pm, _, _ in parts])
    w = [jnp.exp(pm - m) for pm, _, _ in parts]
    l = sum(wi * pl_ for wi, (_, pl_, _) in zip(w, parts))
    o = sum(wi * po for wi, (_, _, po) in zip(w, parts))
    o_ref[...] = (o / l).astype(o_ref.dtype)

    _lru_chunk(lax.rem(b, lru_chunks), lru_chunks, *lru_in, *lru_out, *lru_scr)


def _attn_sample_lru(q, ckv, kpe, cache_kv, cache_kr_t, page_table, xr_p, sgr_p, ws, layer, bp, t):
    bsz, rows, d_qk = q.shape
    t_len = ckv.shape[1]
    kv_rank = ckv.shape[-1]
    page = cache_kv.shape[2]
    n_pages = page_table.shape[1]
    assert n_pages % SAMPLE_SECTIONS == 0
    past = n_pages * page
    tc = (bp * t) // bsz
    assert tc * bsz == bp * t and t % tc == 0 and tc % 8 == 0
    lru_in, lru_out, lru_shape, lru_scratch = _lru_prompt_pieces(xr_p, ws, layer, bp, t, tc,
                                                                 lambda b, pt: b)
    kern = functools.partial(_attn_sample_kernel, layer=layer, n_pages=n_pages, page=page,
                             t_len=t_len, kv_rank=kv_rank, lru_chunks=t // tc)
    o, y, hlast, conv = pl.pallas_call(
        kern,
        grid_spec=pltpu.PrefetchScalarGridSpec(
            num_scalar_prefetch=1,
            grid=(bsz,),
            in_specs=[pl.BlockSpec((None, rows, d_qk), lambda b, pt: (b, 0, 0)),
                      pl.BlockSpec((None, t_len, kv_rank), lambda b, pt: (b, 0, 0)),
                      pl.BlockSpec((None, t_len, QK_ROPE), lambda b, pt: (b, 0, 0)),
                      pl.BlockSpec(memory_space=pl.ANY), pl.BlockSpec(memory_space=pl.ANY)] + lru_in,
            out_specs=[pl.BlockSpec((None, rows, kv_rank), lambda b, pt: (b, 0, 0))] + lru_out,
            scratch_shapes=[pltpu.VMEM((2, past, kv_rank), F32), pltpu.VMEM((2, QK_ROPE, past), F32),
                            pltpu.VMEM((page, kv_rank), F32), pltpu.VMEM((page, QK_ROPE), F32),
                            pltpu.SemaphoreType.DMA((2, 2))] + lru_scratch),
        out_shape=[jax.ShapeDtypeStruct((bsz, rows, kv_rank), BF16)] + lru_shape,
        compiler_params=_params("arbitrary"),
        name="attn_sample_lru",
    )(page_table, q, ckv, kpe, cache_kv, cache_kr_t, xr_p, sgr_p, *[ws[k] for k in LRU_WEIGHTS])
    return o, y, hlast.reshape(bp, -1), conv


def _out_compute(x_ref, yr_ref, o_ref, sga_ref, wuv_ref, wout_ref):
    d_rg = yr_ref.shape[-1]
    sga = sga_ref[...]
    acc = x_ref[...] + _dot(yr_ref[...], wout_ref[0:d_rg, :])
    y_a = [(_dot(o_ref[h], wuv_ref[h]) * sga[:, V_DIM * h:V_DIM * (h + 1)]).astype(BF16)
           for h in range(N_HEADS)]
    return acc + _dot(jnp.concatenate(y_a, axis=-1), wout_ref[d_rg:, :])


def _out_kernel(x_ref, yr_ref, o_ref, sga_ref, wuv_ref, wout_ref, fg_ref, out_ref, *, final):
    acc = _out_compute(x_ref, yr_ref, o_ref, sga_ref, wuv_ref, wout_ref)
    out_ref[...] = _rms(acc, fg_ref[...]) if final else acc


def _out_proj_kernel(*refs, dims):
    out_in, proj_in = refs[:6], refs[6:14]
    xnew_ref, proj_out = refs[14], refs[15:]
    acc = _out_compute(*out_in)
    xnew_ref[...] = acc
    _proj_kernel(None, *proj_in, *proj_out, **dims, x=acc)


def _out_specs(x, yr, o_lat, sga, ws, layer, tm):
    rows = lambda w: pl.BlockSpec((tm, w), lambda i: (i, 0))
    specs = [rows(x.shape[1]), rows(yr.shape[1]),
             pl.BlockSpec((N_HEADS, tm, o_lat.shape[-1]), lambda i: (0, i, 0)), rows(sga.shape[1]),
             _layer_spec(ws["w_uv"], layer), _layer_spec(ws["w_out"], layer)]
    return specs, (x, yr, o_lat, sga, ws["w_uv"], ws["w_out"])


def _out(x, yr, o_lat, sga, ws, layer, final_g, final):
    n, d_model = x.shape
    tm = min(ROW_TILE, n)
    specs, args = _out_specs(x, yr, o_lat, sga, ws, layer, tm)
    return pl.pallas_call(
        functools.partial(_out_kernel, final=final),
        grid=(n // tm,),
        in_specs=specs + [pl.BlockSpec((1, d_model), lambda i: (0, 0))],
        out_specs=pl.BlockSpec((tm, d_model), lambda i: (i, 0)),
        out_shape=jax.ShapeDtypeStruct((n, d_model), F32),
        compiler_params=_params("arbitrary"),
        name="out",
    )(*args, final_g)


def _out_proj(x, yr, o_lat, sga, ws, layer, cos_tab, sin_tab, tab_blocks):
    n, d_model = x.shape
    tm = min(ROW_TILE, n)
    specs, args = _out_specs(x, yr, o_lat, sga, ws, layer, tm)
    in_specs, out_specs, out_shape, dims = _proj_pieces(ws, layer + 1, n, tm, tab_blocks)
    return pl.pallas_call(
        functools.partial(_out_proj_kernel, dims=dims),
        grid=(n // tm,),
        in_specs=specs + in_specs,
        out_specs=[pl.BlockSpec((tm, d_model), lambda i: (i, 0))] + out_specs,
        out_shape=[jax.ShapeDtypeStruct((n, d_model), F32)] + out_shape,
        compiler_params=_params("arbitrary"),
        name="out_proj",
    )(*args, *[ws[k] for k in PROJ_WEIGHTS], cos_tab, sin_tab)


def _rope_tables(pos):
    half = QK_ROPE // 2
    freqs = ROPE_THETA ** (-jnp.arange(half, dtype=F32) / half)
    ang = pos.astype(F32)[:, None] * freqs
    cos, sin = jnp.cos(ang), jnp.sin(ang)
    return jnp.concatenate([cos, cos, cos, cos], axis=-1), jnp.concatenate([-sin, sin, -sin, sin], axis=-1)


def _swap_halves(w):
    half = w.shape[-1] // 2
    return jnp.concatenate([w[..., half:], w[..., :half]], axis=-1)


def _prepare_weights(norm_g, w_in, conv_w, conv_b, w_a, b_a, w_x, b_x, lru_lambda, q_norm_g, w_uq,
                     kv_norm_g, w_uk, w_uv, w_out):
    depth, d_model, _ = w_in.shape
    d_rg = conv_b.shape[-1]
    q_rank = q_norm_g.shape[-1]
    kv_rank = kv_norm_g.shape[-1]
    s3 = 2 * d_rg + q_rank + kv_rank
    s4 = s3 + QK_ROPE
    k_pe_cols = w_in[..., s3:s4]
    w_in2 = jnp.concatenate([w_in[..., :s3], w_in[..., s4:], k_pe_cols, _swap_halves(k_pe_cols)], axis=-1)
    wq = w_uq.reshape(depth, q_rank, N_HEADS, QK_NOPE + QK_ROPE)
    q_pe_cols = wq[..., QK_NOPE:]
    w_uq2 = jnp.concatenate([wq[..., :QK_NOPE].reshape(depth, q_rank, -1),
                             q_pe_cols.reshape(depth, q_rank, -1),
                             _swap_halves(q_pe_cols).reshape(depth, q_rank, -1)], axis=-1)
    eye = jnp.eye(N_RG_HEADS, dtype=w_a.dtype)
    block_diag = lambda w: jnp.einsum("lhij,hg->lhigj", w, eye).reshape(depth, d_rg, d_rg)
    w_gate = jnp.concatenate([block_diag(w_a), block_diag(w_x)], axis=-1)
    row = lambda v: v[:, None, :]
    return dict(
        norm_g=row(norm_g), w_in=w_in2.astype(BF16), conv_w=conv_w, conv_b=row(conv_b),
        w_gate=w_gate.astype(BF16), b_a=row(b_a), b_x=row(b_x), lru_lambda=row(lru_lambda),
        q_norm_g=row(q_norm_g), w_uq=w_uq2.astype(BF16), kv_norm_g=row(kv_norm_g),
        w_ukt=jnp.transpose(w_uk, (0, 2, 3, 1)).astype(BF16),
        w_uv=jnp.transpose(w_uv, (0, 2, 1, 3)).astype(BF16),
        w_out=w_out.astype(BF16))


def kernel(x_prompt, x_sample, cache_kv_latent, cache_k_rope, page_table, state_rglru, state_conv,
           norm_g, w_in, conv_w, conv_b, w_a, b_a, w_x, b_x, lru_lambda, q_norm_g, w_uq,
           kv_norm_g, w_uk, w_uv, w_out, final_norm_g):
    bp, t, d_model = x_prompt.shape
    bs, ts, _ = x_sample.shape
    depth = w_in.shape[0]
    kv_rank = kv_norm_g.shape[-1]
    d_rg = conv_b.shape[-1]
    past_len = page_table.shape[1] * cache_kv_latent.shape[2]

    ws = _prepare_weights(norm_g, w_in, conv_w, conv_b, w_a, b_a, w_x, b_x, lru_lambda,
                          q_norm_g, w_uq, kv_norm_g, w_uk, w_uv, w_out)
    final_g = final_norm_g[None, :]
    cos_p, sin_p = _rope_tables(jnp.arange(t, dtype=jnp.int32))
    pos_s = past_len + jnp.arange(ts, dtype=jnp.int32)
    cos_s, sin_s = _rope_tables(jnp.tile(pos_s, bs))
    tab_blocks_p = t // min(ROW_TILE, bp * t)
    tab_blocks_s = (bs * ts) // min(ROW_TILE, bs * ts)
    buf_t = jnp.transpose(state_conv, (0, 2, 1, 3))
    cache_kr_t = jnp.swapaxes(cache_k_rope, 2, 3)

    x_p = x_prompt.reshape(bp * t, d_model)
    x_s = x_sample.reshape(bs * ts, d_model)
    kv_p, kr_p, h_p, cv_p = [], [], [], []
    kv_s, kr_s, h_s, cv_s = [], [], [], []
    proj_p = _proj(x_p, ws, 0, cos_p, sin_p, tab_blocks_p)
    for l in range(depth):
        final = l == depth - 1

        xr_p, sgr_p, sga_p, q, k, ckv, kpe = proj_p
        o_lat_p = _attn_prompt(q, k, bp, t, kv_rank)
        kv_p.append(ckv.reshape(bp, t, kv_rank)); kr_p.append(kpe.reshape(bp, t, QK_ROPE))

        xr, sgr, sga, q, _, ckv, kpe = _proj(x_s, ws, l, cos_s, sin_s, tab_blocks_s)
        to_time_major = lambda v: jnp.transpose(v.reshape(bs, ts, -1), (1, 0, 2))
        yr_t, hl, cv_t = _rglru_sample(to_time_major(xr), to_time_major(sgr), buf_t, state_rglru, ws, l)
        yr = jnp.transpose(yr_t, (1, 0, 2)).reshape(bs * ts, d_rg)
        ckv = ckv.reshape(bs, ts, kv_rank)
        kpe = kpe.reshape(bs, ts, QK_ROPE)
        q_b = jnp.transpose(q.reshape(N_HEADS, bs, ts, -1), (1, 0, 2, 3)).reshape(bs, N_HEADS * ts, -1)
        o_b, yr_p, hl_p, cv_p_l = _attn_sample_lru(q_b, ckv, kpe, cache_kv_latent, cache_kr_t, page_table,
                                                   xr_p, sgr_p, ws, l, bp, t)
        o_lat = jnp.transpose(o_b.reshape(bs, N_HEADS, ts, kv_rank), (1, 0, 2, 3)).reshape(
            N_HEADS, bs * ts, kv_rank)
        x_s = _out(x_s, yr, o_lat, sga, ws, l, final_g, final)
        kv_s.append(ckv); kr_s.append(kpe); h_s.append(hl); cv_s.append(cv_t)
        h_p.append(hl_p); cv_p.append(cv_p_l)

        if final:
            x_p = _out(x_p, yr_p, o_lat_p, sga_p, ws, l, final_g, True)
        else:
            x_p, *proj_p = _out_proj(x_p, yr_p, o_lat_p, sga_p, ws, l, cos_p, sin_p, tab_blocks_p)

    y_prompt = x_p.reshape(bp, t, d_model)
    y_sample = x_s.reshape(bs, ts, d_model)
    conv_sample = jnp.transpose(jnp.stack(cv_s), (0, 2, 1, 3))
    return (y_prompt, y_sample,
            jnp.stack(kv_p), jnp.stack(kr_p), jnp.stack(h_p), jnp.stack(cv_p),
            jnp.stack(kv_s), jnp.stack(kr_s), jnp.stack(h_s), conv_sample)
```

```python
import functools

import jax
import jax.numpy as jnp
from jax import lax
from jax.experimental import pallas as pl
from jax.experimental.pallas import tpu as pltpu

F32 = jnp.float32
BF16 = jnp.bfloat16

N_RG_HEADS = 8
CONV_W = 4
LRU_C = 8.0
N_HEADS = 4
QK_NOPE = 128
QK_ROPE = 64
V_DIM = 128
ROPE_THETA = 10000.0
ATT_SCALE = (QK_NOPE + QK_ROPE) ** -0.5
EPS = 1e-6
MASK_VALUE = -1e30

VMEM_LIMIT_BYTES = 48 * 1024 * 1024

ROW_TILE = 512
ATT_TQ = 512
SAMPLE_SECTIONS = 2
PAGE_BUFFERS = 3
LANE = 128


def _params(*sem):
    return pltpu.CompilerParams(dimension_semantics=sem, vmem_limit_bytes=VMEM_LIMIT_BYTES)


def _layer_spec(w, layer):
    zeros = (0,) * (w.ndim - 1)
    return pl.BlockSpec((None,) + w.shape[1:], lambda *grid: (layer,) + zeros)


def _rms(v, g):
    return v * lax.rsqrt(jnp.mean(v * v, axis=-1, keepdims=True) + EPS) * g


def _silu(v):
    return v * jax.nn.sigmoid(v)


def _dot(a, b):
    return jnp.dot(a, b, preferred_element_type=F32)


def _dot_nt(a, b):
    return lax.dot_general(a, b, (((1,), (1,)), ((), ())), preferred_element_type=F32)


def _proj_kernel(x_ref, ng_ref, win_ref, qng_ref, wuq_ref, kvng_ref, wukt_ref, cos_ref, sin_ref,
                 xr_ref, sgr_ref, sga_ref, q_ref, k_ref, ckv_ref, kpe_ref, *, d_rg, q_rank, kv_rank,
                 x=None):
    u = _rms(x_ref[...] if x is None else x, ng_ref[...])
    z = _dot(u.astype(BF16), win_ref[...])
    o = 0
    xr_ref[...] = z[:, o:o + d_rg]; o += d_rg
    sgr_ref[...] = _silu(z[:, o:o + d_rg]); o += d_rg
    c_q = z[:, o:o + q_rank]; o += q_rank
    c_kv = z[:, o:o + kv_rank]; o += kv_rank
    d_att = N_HEADS * V_DIM
    sga_ref[...] = _silu(z[:, o:o + d_att]); o += d_att
    kpe_raw = z[:, o:o + QK_ROPE]; o += QK_ROPE
    kpe_swp = z[:, o:o + QK_ROPE]

    cos = cos_ref[...]
    sin = sin_ref[...]

    ckv_n = _rms(c_kv, kvng_ref[...])
    kpe = kpe_raw * cos[:, :QK_ROPE] + kpe_swp * sin[:, :QK_ROPE]
    ckv_ref[...] = ckv_n
    kpe_ref[...] = kpe
    rope_pad = jnp.zeros((u.shape[0], LANE - QK_ROPE), BF16)
    k_ref[:, 0:kv_rank] = ckv_n.astype(BF16)
    k_ref[:, kv_rank:] = jnp.concatenate([kpe.astype(BF16), rope_pad], axis=-1)

    q = _dot(_rms(c_q, qng_ref[...]).astype(BF16), wuq_ref[...])
    pe0 = N_HEADS * QK_NOPE
    sw0 = pe0 + N_HEADS * QK_ROPE
    for pair in range(N_HEADS // 2):
        pe = q[:, pe0 + 128 * pair:pe0 + 128 * (pair + 1)]
        sw = q[:, sw0 + 128 * pair:sw0 + 128 * (pair + 1)]
        pe_r = (pe * cos + sw * sin) * ATT_SCALE
        for sub in range(2):
            h = 2 * pair + sub
            q_ref[h, :, kv_rank:] = jnp.concatenate(
                [pe_r[:, QK_ROPE * sub:QK_ROPE * (sub + 1)].astype(BF16), rope_pad], axis=-1)
    for h in range(N_HEADS):
        q_lat = _dot(q[:, QK_NOPE * h:QK_NOPE * (h + 1)].astype(BF16), wukt_ref[h])
        q_ref[h, :, 0:kv_rank] = (q_lat * ATT_SCALE).astype(BF16)


PROJ_WEIGHTS = ("norm_g", "w_in", "q_norm_g", "w_uq", "kv_norm_g", "w_ukt")


def _proj_pieces(ws, layer, n, tm, tab_blocks):
    d_rg = ws["conv_b"].shape[-1]
    q_rank = ws["q_norm_g"].shape[-1]
    kv_rank = ws["kv_norm_g"].shape[-1]
    d_att = N_HEADS * V_DIM
    d_qk = kv_rank + LANE
    rows = lambda w: pl.BlockSpec((tm, w), lambda i: (i, 0))
    tab = pl.BlockSpec((tm, LANE), lambda i: (i % tab_blocks, 0))
    in_specs = [_layer_spec(ws[k], layer) for k in PROJ_WEIGHTS] + [tab, tab]
    out_specs = [rows(d_rg), rows(d_rg), rows(d_att),
                 pl.BlockSpec((N_HEADS, tm, d_qk), lambda i: (0, i, 0)),
                 rows(d_qk), rows(kv_rank), rows(QK_ROPE)]
    out_shape = [jax.ShapeDtypeStruct((n, d_rg), F32), jax.ShapeDtypeStruct((n, d_rg), F32),
                 jax.ShapeDtypeStruct((n, d_att), F32),
                 jax.ShapeDtypeStruct((N_HEADS, n, d_qk), BF16),
                 jax.ShapeDtypeStruct((n, d_qk), BF16),
                 jax.ShapeDtypeStruct((n, kv_rank), F32), jax.ShapeDtypeStruct((n, QK_ROPE), F32)]
    return in_specs, out_specs, out_shape, dict(d_rg=d_rg, q_rank=q_rank, kv_rank=kv_rank)


def _proj(x, ws, layer, cos_tab, sin_tab, tab_blocks):
    n, d_model = x.shape
    tm = min(ROW_TILE, n)
    in_specs, out_specs, out_shape, dims = _proj_pieces(ws, layer, n, tm, tab_blocks)
    return pl.pallas_call(
        functools.partial(_proj_kernel, **dims),
        grid=(n // tm,),
        in_specs=[pl.BlockSpec((tm, d_model), lambda i: (i, 0))] + in_specs,
        out_specs=out_specs,
        out_shape=out_shape,
        compiler_params=_params("arbitrary"),
        name="proj",
    )(x, *[ws[k] for k in PROJ_WEIGHTS], cos_tab, sin_tab)


LRU_WEIGHTS = ("conv_w", "conv_b", "w_gate", "b_a", "b_x", "lru_lambda")


def _lru_gates(xc, wg_ref, ba_ref, bx_ref, lam_ref):
    d_rg = xc.shape[-1]
    g = _dot(xc.astype(BF16), wg_ref[...])
    r = jax.nn.sigmoid(g[:, :d_rg] + ba_ref[...])
    i = jax.nn.sigmoid(g[:, d_rg:] + bx_ref[...])
    neg_lam = -lam_ref[...]
    softplus = jnp.maximum(neg_lam, 0.0) + jnp.log1p(jnp.exp(-jnp.abs(neg_lam)))
    log_a = -LRU_C * r * softplus
    a = jnp.exp(log_a)
    return a, jnp.sqrt(-jnp.tanh(log_a) * (a * a + 1.0)) * i * xc


def _lru_chunk(ci, nc, xr_ref, sgr_ref, cw_ref, cb_ref, wg_ref, ba_ref, bx_ref, lam_ref,
               y_ref, hlast_ref, conv_ref, xpad_scr, a_scr, b_scr, h_scr):
    tc, d_rg = xr_ref.shape

    @pl.when(ci == 0)
    def _():
        xpad_scr[0:8, :] = jnp.zeros((8, d_rg), F32)
        h_scr[...] = jnp.zeros((8, d_rg), F32)

    xpad_scr[8:8 + tc, :] = xr_ref[...]
    xc = cb_ref[...] + cw_ref[0:1, :] * xpad_scr[5:5 + tc, :]
    for k in range(1, CONV_W):
        xc = xc + cw_ref[k:k + 1, :] * xpad_scr[5 + k:5 + k + tc, :]
    a, b = _lru_gates(xc, wg_ref, ba_ref, bx_ref, lam_ref)

    a = a.reshape(tc // 8, 8, d_rg)
    b = b.reshape(tc // 8, 8, d_rg)
    step = lax.broadcasted_iota(jnp.int32, a.shape, 1)
    for d in (1, 2, 4):
        keep = step >= d
        b = jnp.where(keep, a * pltpu.roll(b, d, 1) + b, b)
        a = jnp.where(keep, a * pltpu.roll(a, d, 1), a)
    a_scr[...] = a.reshape(tc, d_rg)
    b_scr[...] = b.reshape(tc, d_rg)

    def group(g, h_in):
        r0 = pl.multiple_of(g * 8, 8)
        hs = a_scr[pl.ds(r0, 8), :] * h_in + b_scr[pl.ds(r0, 8), :]
        b_scr[pl.ds(r0, 8), :] = hs
        return jnp.broadcast_to(hs[7:8, :], hs.shape)

    h_in = lax.fori_loop(0, tc // 8, group, h_scr[...], unroll=8)
    h_scr[...] = h_in
    y_ref[...] = (b_scr[...] * sgr_ref[...]).astype(y_ref.dtype)
    xpad_scr[0:8, :] = xpad_scr[tc:tc + 8, :]

    @pl.when(ci == nc - 1)
    def _():
        hlast_ref[...] = h_in[0:1, :]
        conv_ref[...] = xpad_scr[tc + 8 - (CONV_W - 1):tc + 8, :]


def _lru_prompt_pieces(xr, ws, layer, bsz, t, tc, step_index):
    d_rg = xr.shape[-1]
    nc = t // tc
    rows = pl.BlockSpec((tc, d_rg), lambda *g: (step_index(*g), 0))
    seq = lambda w: pl.BlockSpec((None, w, d_rg), lambda *g: (step_index(*g) // nc, 0, 0))
    in_specs = [rows, rows] + [_layer_spec(ws[k], layer) for k in LRU_WEIGHTS]
    out_specs = [rows, seq(1), seq(CONV_W - 1)]
    out_shape = [jax.ShapeDtypeStruct((bsz * t, d_rg), BF16),
                 jax.ShapeDtypeStruct((bsz, 1, d_rg), F32),
                 jax.ShapeDtypeStruct((bsz, CONV_W - 1, d_rg), F32)]
    scratch = [pltpu.VMEM((tc + 8, d_rg), F32), pltpu.VMEM((tc, d_rg), F32),
               pltpu.VMEM((tc, d_rg), F32), pltpu.VMEM((8, d_rg), F32)]
    return in_specs, out_specs, out_shape, scratch


def _rglru_sample_kernel(xr_ref, sgr_ref, buf_ref, h0_ref, cw_ref, cb_ref, wg_ref, ba_ref, bx_ref,
                         lam_ref, y_ref, hlast_ref, conv_ref):
    t_len = xr_ref.shape[0]
    xp = [buf_ref[k] for k in range(CONV_W - 1)] + [xr_ref[t] for t in range(t_len)]
    h = h0_ref[...]
    for t in range(t_len):
        xc = cb_ref[...] + cw_ref[0:1, :] * xp[t]
        for k in range(1, CONV_W):
            xc = xc + cw_ref[k:k + 1, :] * xp[t + k]
        a, b = _lru_gates(xc, wg_ref, ba_ref, bx_ref, lam_ref)
        h = a * h + b
        y_ref[t] = (h * sgr_ref[t]).astype(y_ref.dtype)
    hlast_ref[...] = h
    for k in range(CONV_W - 1):
        conv_ref[k] = xp[t_len + k]


def _rglru_sample(xr_t, sgr_t, buf_t, h0, ws, layer):
    t_len, bsz, d_rg = xr_t.shape
    whole = lambda shape: pl.BlockSpec(shape, lambda i: (0,) * len(shape))
    return pl.pallas_call(
        _rglru_sample_kernel,
        grid=(1,),
        in_specs=[whole(xr_t.shape), whole(sgr_t.shape), _layer_spec(buf_t, layer), _layer_spec(h0, layer)]
        + [_layer_spec(ws[k], layer) for k in LRU_WEIGHTS],
        out_specs=[whole((t_len, bsz, d_rg)), whole((bsz, d_rg)), whole((CONV_W - 1, bsz, d_rg))],
        out_shape=[jax.ShapeDtypeStruct((t_len, bsz, d_rg), BF16),
                   jax.ShapeDtypeStruct((bsz, d_rg), F32),
                   jax.ShapeDtypeStruct((CONV_W - 1, bsz, d_rg), F32)],
        compiler_params=_params("arbitrary"),
        name="rglru_sample",
    )(xr_t, sgr_t, buf_t, h0, *[ws[k] for k in LRU_WEIGHTS])


def _attn_prompt_kernel(q_ref, k_ref, o_ref, qt_scr, vt_scr, s_scr, m_scr, l_scr, acc_scr, *, tq, kv_rank):
    qi = pl.program_id(1)
    t = k_ref.shape[0]

    @pl.when(qi == 0)
    def _():
        for c in range(t // tq):
            blk = k_ref[c * tq:(c + 1) * tq, 0:kv_rank]
            vt_scr[:, c * tq:(c + 1) * tq] = blk.astype(F32).T.astype(BF16)

    for h in range(N_HEADS):
        qt_scr[:, h * tq:(h + 1) * tq] = q_ref[h].astype(F32).T.astype(BF16)
    m_scr[...] = jnp.full(m_scr.shape, MASK_VALUE, F32)
    l_scr[...] = jnp.zeros(l_scr.shape, F32)
    acc_scr[...] = jnp.zeros(acc_scr.shape, F32)

    def scores(ki):
        kb = k_ref[pl.ds(pl.multiple_of(ki * tq, tq), tq), :]
        return _dot(kb, qt_scr[...])

    def update(ki, s):
        m_prev = m_scr[...]
        m_new = jnp.maximum(m_prev, jnp.max(s, axis=0, keepdims=True))
        alpha = jnp.exp(m_prev - m_new)
        p = jnp.exp(s - m_new)
        l_scr[...] = alpha * l_scr[...] + jnp.sum(p, axis=0, keepdims=True)
        vt = vt_scr[:, pl.ds(pl.multiple_of(ki * tq, tq), tq)]
        acc_scr[...] = alpha * acc_scr[...] + _dot(vt, p.astype(BF16))
        m_scr[...] = m_new

    s_scr[...] = scores(0)

    def body(ki, c):
        s_next = scores(ki + 1)
        update(ki, s_scr[...])
        s_scr[...] = s_next
        return c

    lax.fori_loop(0, qi, body, 0)
    s = s_scr[...]
    k_off = lax.broadcasted_iota(jnp.int32, s.shape, 0)
    q_off = lax.broadcasted_iota(jnp.int32, s.shape, 1) & (tq - 1)
    update(qi, jnp.where(k_off <= q_off, s, MASK_VALUE))

    o_t = acc_scr[...] / l_scr[...]
    for h in range(N_HEADS):
        o_ref[h] = o_t[:, h * tq:(h + 1) * tq].T.astype(o_ref.dtype)


def _attn_prompt(q, k, bsz, t, kv_rank):
    d_qk = q.shape[-1]
    tq = min(ATT_TQ, t)
    assert tq & (tq - 1) == 0 and t % tq == 0
    nq = t // tq
    cols = N_HEADS * tq
    return pl.pallas_call(
        functools.partial(_attn_prompt_kernel, tq=tq, kv_rank=kv_rank),
        grid=(bsz, nq),
        in_specs=[pl.BlockSpec((N_HEADS, tq, d_qk), lambda b, i: (0, b * nq + i, 0)),
                  pl.BlockSpec((t, d_qk), lambda b, i: (b, 0))],
        out_specs=pl.BlockSpec((N_HEADS, tq, kv_rank), lambda b, i: (0, b * nq + i, 0)),
        out_shape=jax.ShapeDtypeStruct((N_HEADS, bsz * t, kv_rank), BF16),
        scratch_shapes=[pltpu.VMEM((d_qk, cols), BF16), pltpu.VMEM((kv_rank, t), BF16),
                        pltpu.VMEM((tq, cols), F32), pltpu.VMEM((1, cols), F32),
                        pltpu.VMEM((1, cols), F32), pltpu.VMEM((kv_rank, cols), F32)],
        compiler_params=_params("arbitrary", "arbitrary"),
        name="attn_prompt",
    )(q, k)


def _attn_sample_kernel(pt_ref, *refs, layer, n_pages, page, t_len, kv_rank, lru_chunks):
    q_ref, ccur_ref, pcur_ref, kv_hbm, kr_hbm = refs[:5]
    lru_in, o_ref, lru_out = refs[5:13], refs[13], refs[14:17]
    kv_buf, kr_buf, cc_scr, pc_scr, sem = refs[17:22]
    lru_scr = refs[22:]
    b = pl.program_id(0)
    slot = lax.rem(b, PAGE_BUFFERS)

    def page_copies(seq, sl, j):
        pg = pt_ref[seq, j]
        off = j * page
        return (pltpu.make_async_copy(kv_hbm.at[layer, pg], kv_buf.at[sl, pl.ds(off, page), :], sem.at[sl, 0]),
                pltpu.make_async_copy(kr_hbm.at[layer, pg], kr_buf.at[sl, :, pl.ds(off, page)], sem.at[sl, 1]))

    def fetch(seq, sl):
        for j in range(n_pages):
            for cp in page_copies(seq, sl, j):
                cp.start()

    def wait_all(sl):
        pltpu.make_async_copy(kv_buf.at[sl], kv_buf.at[sl], sem.at[sl, 0]).wait()
        pltpu.make_async_copy(kr_buf.at[sl], kr_buf.at[sl], sem.at[sl, 1]).wait()

    @pl.when(b == 0)
    def _():
        for s0 in range(PAGE_BUFFERS - 1):
            fetch(s0, s0)

    wait_all(slot)
    ahead = b + (PAGE_BUFFERS - 1)

    @pl.when(ahead < pl.num_programs(0))
    def _():
        fetch(ahead, lax.rem(ahead, PAGE_BUFFERS))

    q = q_ref[...]
    q_lat = q[:, :kv_rank]
    q_pe = q[:, kv_rank:kv_rank + QK_ROPE]

    def local(s, v):
        m = jnp.max(s, axis=-1, keepdims=True)
        p = jnp.exp(s - m)
        return m, jnp.sum(p, axis=-1, keepdims=True), _dot(p.astype(BF16), v)

    cc_scr[...] = jnp.zeros(cc_scr.shape, F32)
    pc_scr[...] = jnp.zeros(pc_scr.shape, F32)
    cc_scr[0:t_len, :] = ccur_ref[...]
    pc_scr[0:t_len, :] = pcur_ref[...]
    vc = cc_scr[...].astype(BF16)
    s_cur = _dot_nt(q_lat, vc) + _dot_nt(q_pe, pc_scr[...].astype(BF16))
    q_step = lax.rem(lax.broadcasted_iota(jnp.int32, s_cur.shape, 0), t_len)
    k_step = lax.broadcasted_iota(jnp.int32, s_cur.shape, 1)
    s_cur = jnp.where(k_step <= q_step, s_cur, MASK_VALUE)

    parts = [local(s_cur, vc)]
    sec = (n_pages // SAMPLE_SECTIONS) * page
    for i in range(SAMPLE_SECTIONS):
        r = slice(i * sec, (i + 1) * sec)
        kc = kv_buf[slot, r, :]
        s = _dot(q_lat, kc.T.astype(BF16)) + _dot(q_pe, kr_buf[slot, :, r].astype(BF16))
        parts.append(local(s, kc.astype(BF16)))
    m = functools.reduce(jnp.maximum, [pm for pm, _, _ in parts])
    w = [jnp.exp(pm - m) for pm, _, _ in parts]
    l = sum(wi * pl_ for wi, (_, pl_, _) in zip(w, parts))
    o = sum(wi * po for wi, (_, _, po) in zip(w, parts))
    o_ref[...] = (o / l).astype(o_ref.dtype)

    _lru_chunk(lax.rem(b, lru_chunks), lru_chunks, *lru_in, *lru_out, *lru_scr)


def _attn_sample_lru(q, ckv, kpe, cache_kv, cache_kr_t, page_table, xr_p, sgr_p, ws, layer, bp, t):
    bsz, rows, d_qk = q.shape
    t_len = ckv.shape[1]
    kv_rank = ckv.shape[-1]
    page = cache_kv.shape[2]
    n_pages = page_table.shape[1]
    assert n_pages % SAMPLE_SECTIONS == 0
    past = n_pages * page
    tc = (bp * t) // bsz
    assert tc * bsz == bp * t and t % tc == 0 and tc % 8 == 0
    lru_in, lru_out, lru_shape, lru_scratch = _lru_prompt_pieces(xr_p, ws, layer, bp, t, tc,
                                                                 lambda b, pt: b)
    kern = functools.partial(_attn_sample_kernel, layer=layer, n_pages=n_pages, page=page,
                             t_len=t_len, kv_rank=kv_rank, lru_chunks=t // tc)
    o, y, hlast, conv = pl.pallas_call(
        kern,
        grid_spec=pltpu.PrefetchScalarGridSpec(
            num_scalar_prefetch=1,
            grid=(bsz,),
            in_specs=[pl.BlockSpec((None, rows, d_qk), lambda b, pt: (b, 0, 0)),
                      pl.BlockSpec((None, t_len, kv_rank), lambda b, pt: (b, 0, 0)),
                      pl.BlockSpec((None, t_len, QK_ROPE), lambda b, pt: (b, 0, 0)),
                      pl.BlockSpec(memory_space=pl.ANY), pl.BlockSpec(memory_space=pl.ANY)] + lru_in,
            out_specs=[pl.BlockSpec((None, rows, kv_rank), lambda b, pt: (b, 0, 0))] + lru_out,
            scratch_shapes=[pltpu.VMEM((PAGE_BUFFERS, past, kv_rank), F32),
                            pltpu.VMEM((PAGE_BUFFERS, QK_ROPE, past), F32),
                            pltpu.VMEM((page, kv_rank), F32), pltpu.VMEM((page, QK_ROPE), F32),
                            pltpu.SemaphoreType.DMA((PAGE_BUFFERS, 2))] + lru_scratch),
        out_shape=[jax.ShapeDtypeStruct((bsz, rows, kv_rank), BF16)] + lru_shape,
        compiler_params=_params("arbitrary"),
        name="attn_sample_lru",
    )(page_table, q, ckv, kpe, cache_kv, cache_kr_t, xr_p, sgr_p, *[ws[k] for k in LRU_WEIGHTS])
    return o, y, hlast.reshape(bp, -1), conv


def _out_compute(x_ref, yr_ref, o_ref, sga_ref, wuv_ref, wout_ref):
    d_rg = yr_ref.shape[-1]
    sga = sga_ref[...]
    acc = x_ref[...] + _dot(yr_ref[...], wout_ref[0:d_rg, :])
    y_a = [(_dot(o_ref[h], wuv_ref[h]) * sga[:, V_DIM * h:V_DIM * (h + 1)]).astype(BF16)
           for h in range(N_HEADS)]
    return acc + _dot(jnp.concatenate(y_a, axis=-1), wout_ref[d_rg:, :])


def _out_kernel(x_ref, yr_ref, o_ref, sga_ref, wuv_ref, wout_ref, fg_ref, out_ref, *, final):
    acc = _out_compute(x_ref, yr_ref, o_ref, sga_ref, wuv_ref, wout_ref)
    out_ref[...] = _rms(acc, fg_ref[...]) if final else acc


def _out_proj_kernel(*refs, dims):
    out_in, proj_in = refs[:6], refs[6:14]
    xnew_ref, proj_out = refs[14], refs[15:]
    acc = _out_compute(*out_in)
    xnew_ref[...] = acc
    _proj_kernel(None, *proj_in, *proj_out, **dims, x=acc)


def _out_specs(x, yr, o_lat, sga, ws, layer, tm):
    rows = lambda w: pl.BlockSpec((tm, w), lambda i: (i, 0))
    specs = [rows(x.shape[1]), rows(yr.shape[1]),
             pl.BlockSpec((N_HEADS, tm, o_lat.shape[-1]), lambda i: (0, i, 0)), rows(sga.shape[1]),
             _layer_spec(ws["w_uv"], layer), _layer_spec(ws["w_out"], layer)]
    return specs, (x, yr, o_lat, sga, ws["w_uv"], ws["w_out"])


def _out(x, yr, o_lat, sga, ws, layer, final_g, final):
    n, d_model = x.shape
    tm = min(ROW_TILE, n)
    specs, args = _out_specs(x, yr, o_lat, sga, ws, layer, tm)
    return pl.pallas_call(
        functools.partial(_out_kernel, final=final),
        grid=(n // tm,),
        in_specs=specs + [pl.BlockSpec((1, d_model), lambda i: (0, 0))],
        out_specs=pl.BlockSpec((tm, d_model), lambda i: (i, 0)),
        out_shape=jax.ShapeDtypeStruct((n, d_model), F32),
        compiler_params=_params("arbitrary"),
        name="out",
    )(*args, final_g)


def _out_proj(x, yr, o_lat, sga, ws, layer, cos_tab, sin_tab, tab_blocks):
    n, d_model = x.shape
    tm = min(ROW_TILE, n)
    specs, args = _out_specs(x, yr, o_lat, sga, ws, layer, tm)
    in_specs, out_specs, out_shape, dims = _proj_pieces(ws, layer + 1, n, tm, tab_blocks)
    return pl.pallas_call(
        functools.partial(_out_proj_kernel, dims=dims),
        grid=(n // tm,),
        in_specs=specs + in_specs,
        out_specs=[pl.BlockSpec((tm, d_model), lambda i: (i, 0))] + out_specs,
        out_shape=[jax.ShapeDtypeStruct((n, d_model), F32)] + out_shape,
        compiler_params=_params("arbitrary"),
        name="out_proj",
    )(*args, *[ws[k] for k in PROJ_WEIGHTS], cos_tab, sin_tab)


def _rope_tables(pos):
    half = QK_ROPE // 2
    freqs = ROPE_THETA ** (-jnp.arange(half, dtype=F32) / half)
    ang = pos.astype(F32)[:, None] * freqs
    cos, sin = jnp.cos(ang), jnp.sin(ang)
    return jnp.concatenate([cos, cos, cos, cos], axis=-1), jnp.concatenate([-sin, sin, -sin, sin], axis=-1)


def _swap_halves(w):
    half = w.shape[-1] // 2
    return jnp.concatenate([w[..., half:], w[..., :half]], axis=-1)


def _prepare_weights(norm_g, w_in, conv_w, conv_b, w_a, b_a, w_x, b_x, lru_lambda, q_norm_g, w_uq,
                     kv_norm_g, w_uk, w_uv, w_out):
    depth, d_model, _ = w_in.shape
    d_rg = conv_b.shape[-1]
    q_rank = q_norm_g.shape[-1]
    kv_rank = kv_norm_g.shape[-1]
    s3 = 2 * d_rg + q_rank + kv_rank
    s4 = s3 + QK_ROPE
    k_pe_cols = w_in[..., s3:s4]
    w_in2 = jnp.concatenate([w_in[..., :s3], w_in[..., s4:], k_pe_cols, _swap_halves(k_pe_cols)], axis=-1)
    wq = w_uq.reshape(depth, q_rank, N_HEADS, QK_NOPE + QK_ROPE)
    q_pe_cols = wq[..., QK_NOPE:]
    w_uq2 = jnp.concatenate([wq[..., :QK_NOPE].reshape(depth, q_rank, -1),
                             q_pe_cols.reshape(depth, q_rank, -1),
                             _swap_halves(q_pe_cols).reshape(depth, q_rank, -1)], axis=-1)
    eye = jnp.eye(N_RG_HEADS, dtype=w_a.dtype)
    block_diag = lambda w: jnp.einsum("lhij,hg->lhigj", w, eye).reshape(depth, d_rg, d_rg)
    w_gate = jnp.concatenate([block_diag(w_a), block_diag(w_x)], axis=-1)
    row = lambda v: v[:, None, :]
    return dict(
        norm_g=row(norm_g), w_in=w_in2.astype(BF16), conv_w=conv_w, conv_b=row(conv_b),
        w_gate=w_gate.astype(BF16), b_a=row(b_a), b_x=row(b_x), lru_lambda=row(lru_lambda),
        q_norm_g=row(q_norm_g), w_uq=w_uq2.astype(BF16), kv_norm_g=row(kv_norm_g),
        w_ukt=jnp.transpose(w_uk, (0, 2, 3, 1)).astype(BF16),
        w_uv=jnp.transpose(w_uv, (0, 2, 1, 3)).astype(BF16),
        w_out=w_out.astype(BF16))


def kernel(x_prompt, x_sample, cache_kv_latent, cache_k_rope, page_table, state_rglru, state_conv,
           norm_g, w_in, conv_w, conv_b, w_a, b_a, w_x, b_x, lru_lambda, q_norm_g, w_uq,
           kv_norm_g, w_uk, w_uv, w_out, final_norm_g):
    bp, t, d_model = x_prompt.shape
    bs, ts, _ = x_sample.shape
    depth = w_in.shape[0]
    kv_rank = kv_norm_g.shape[-1]
    d_rg = conv_b.shape[-1]
    past_len = page_table.shape[1] * cache_kv_latent.shape[2]

    ws = _prepare_weights(norm_g, w_in, conv_w, conv_b, w_a, b_a, w_x, b_x, lru_lambda,
                          q_norm_g, w_uq, kv_norm_g, w_uk, w_uv, w_out)
    final_g = final_norm_g[None, :]
    cos_p, sin_p = _rope_tables(jnp.arange(t, dtype=jnp.int32))
    pos_s = past_len + jnp.arange(ts, dtype=jnp.int32)
    cos_s, sin_s = _rope_tables(jnp.tile(pos_s, bs))
    tab_blocks_p = t // min(ROW_TILE, bp * t)
    tab_blocks_s = (bs * ts) // min(ROW_TILE, bs * ts)
    buf_t = jnp.transpose(state_conv, (0, 2, 1, 3))
    cache_kr_t = jnp.swapaxes(cache_k_rope, 2, 3)

    x_p = x_prompt.reshape(bp * t, d_model)
    x_s = x_sample.reshape(bs * ts, d_model)
    kv_p, kr_p, h_p, cv_p = [], [], [], []
    kv_s, kr_s, h_s, cv_s = [], [], [], []
    proj_p = _proj(x_p, ws, 0, cos_p, sin_p, tab_blocks_p)
    for l in range(depth):
        final = l == depth - 1

        xr_p, sgr_p, sga_p, q, k, ckv, kpe = proj_p
        o_lat_p = _attn_prompt(q, k, bp, t, kv_rank)
        kv_p.append(ckv.reshape(bp, t, kv_rank)); kr_p.append(kpe.reshape(bp, t, QK_ROPE))

        xr, sgr, sga, q, _, ckv, kpe = _proj(x_s, ws, l, cos_s, sin_s, tab_blocks_s)
        to_time_major = lambda v: jnp.transpose(v.reshape(bs, ts, -1), (1, 0, 2))
        yr_t, hl, cv_t = _rglru_sample(to_time_major(xr), to_time_major(sgr), buf_t, state_rglru, ws, l)
        yr = jnp.transpose(yr_t, (1, 0, 2)).reshape(bs * ts, d_rg)
        ckv = ckv.reshape(bs, ts, kv_rank)
        kpe = kpe.reshape(bs, ts, QK_ROPE)
        q_b = jnp.transpose(q.reshape(N_HEADS, bs, ts, -1), (1, 0, 2, 3)).reshape(bs, N_HEADS * ts, -1)
        o_b, yr_p, hl_p, cv_p_l = _attn_sample_lru(q_b, ckv, kpe, cache_kv_latent, cache_kr_t, page_table,
                                                   xr_p, sgr_p, ws, l, bp, t)
        o_lat = jnp.transpose(o_b.reshape(bs, N_HEADS, ts, kv_rank), (1, 0, 2, 3)).reshape(
            N_HEADS, bs * ts, kv_rank)
        x_s = _out(x_s, yr, o_lat, sga, ws, l, final_g, final)
        kv_s.append(ckv); kr_s.append(kpe); h_s.append(hl); cv_s.append(cv_t)
        h_p.append(hl_p); cv_p.append(cv_p_l)

        if final:
            x_p = _out(x_p, yr_p, o_lat_p, sga_p, ws, l, final_g, True)
        else:
            x_p, *proj_p = _out_proj(x_p, yr_p, o_lat_p, sga_p, ws, l, cos_p, sin_p, tab_blocks_p)

    y_prompt = x_p.reshape(bp, t, d_model)
    y_sample = x_s.reshape(bs, ts, d_model)
    conv_sample = jnp.transpose(jnp.stack(cv_s), (0, 2, 1, 3))
    return (y_prompt, y_sample,
            jnp.stack(kv_p), jnp.stack(kr_p), jnp.stack(h_p), jnp.stack(cv_p),
            jnp.stack(kv_s), jnp.stack(kr_s), jnp.stack(h_s), conv_sample)
```
